```python
import jax, jax.numpy as jnp
from jax import lax
import numpy as np

D_MODEL = 1024
BATCH = 8
SEQ = 8192
DEPTH = 2

GRID_W = 64
CTX_LEN = 256
M_HEADS = 4
M_HEAD_DIM = D_MODEL // 8
M_WIDTH = M_HEADS * M_HEAD_DIM
M_CONV = 5
MLSTM_CHUNK = 64
A_HEADS = 4
A_NOPE = D_MODEL // 8
A_ROPE = D_MODEL // 16
A_V = D_MODEL // 8
A_WIDTH = A_HEADS * A_V
Q_LORA = 3 * D_MODEL // 8
KV_LORA = D_MODEL // 4
A_SCALE = (A_NOPE + A_ROPE) ** -0.5
ROPE_BASE = 10000.0
Q_BLOCK = 128
MIX_WIDTH = M_WIDTH + A_WIDTH
OFF_Q = 0
OFF_K = M_WIDTH
OFF_V = 2 * M_WIDTH
OFF_O = 3 * M_WIDTH
OFF_G = 4 * M_WIDTH
OFF_CQ = OFF_G + 4 * M_HEADS
OFF_CKV = OFF_CQ + Q_LORA
OFF_KR = OFF_CKV + KV_LORA
IN_COLS = OFF_KR + A_ROPE
N_EXPERTS = 16
EXPERT_FF = D_MODEL
EC_CAPACITY = 2
ALPHA = (2 * DEPTH) ** 0.25
BETA = (8 * DEPTH) ** -0.25

kernel_name = 'hybrid_mlstm_mla_ecmoe_dit'


def layer_norm(x, eps=1e-6):
    xf = x.astype(jnp.float32)
    mu = jnp.mean(xf, axis=-1, keepdims=True)
    var = jnp.mean(jnp.square(xf - mu), axis=-1, keepdims=True)
    return ((xf - mu) * lax.rsqrt(var + eps)).astype(x.dtype)


def rms_norm(x, g, eps=1e-6):
    xf = x.astype(jnp.float32)
    return (xf * lax.rsqrt(jnp.mean(jnp.square(xf), axis=-1, keepdims=True) + eps)).astype(x.dtype) * g


def modulate(x, shift, scale):
    return layer_norm(x) * (1.0 + scale) + shift


def post_norm(x, y, g, b):
    return layer_norm(ALPHA * x + y, eps=1e-5) * g + b


def split_heads(a, nh):
    B, T, _ = a.shape
    return a.reshape(B, T, nh, -1).transpose(0, 2, 1, 3)


def flip_t(a):
    return jnp.flip(a, axis=2)


def short_conv(x, w, b):
    y = lax.conv_general_dilated(x, w[:, None, :].astype(x.dtype), window_strides=(1,), padding='SAME',
                                 dimension_numbers=('NWC', 'WIO', 'NWC'), feature_group_count=x.shape[-1])
    return jax.nn.silu(y + b)


def axial_rope(T):
    ROWS = T // GRID_W
    half = A_ROPE // 2
    inv = ROPE_BASE ** (-jnp.arange(0, half, 2, dtype=jnp.float32) / half)
    row = jnp.repeat(jnp.arange(ROWS, dtype=jnp.float32), GRID_W)
    col = jnp.tile(jnp.arange(GRID_W, dtype=jnp.float32), ROWS)
    ang = jnp.concatenate([row[:, None] * inv, col[:, None] * inv], axis=-1)
    return jnp.cos(ang), jnp.sin(ang)


def apply_rope(x, cos, sin):
    half = A_ROPE // 2
    cos = cos.astype(x.dtype)
    sin = sin.astype(x.dtype)
    x1, x2 = x[..., :half], x[..., half:]
    return jnp.concatenate([x1 * cos - x2 * sin, x2 * cos + x1 * sin], axis=-1)


def zero_state(B):
    f32 = jnp.float32
    return (jnp.zeros((B, M_HEADS, M_HEAD_DIM, M_HEAD_DIM), f32), jnp.zeros((B, M_HEADS, M_HEAD_DIM), f32),
            jnp.zeros((B, M_HEADS), f32))


def mlstm_chunkwise(q, k, v, i_pre, f_pre, state):
    B, H, T, DH = q.shape
    L = MLSTM_CHUNK
    nc = T // L
    f32 = jnp.float32

    def chunks(a):
        return jnp.moveaxis(a.reshape(a.shape[:2] + (nc, L) + a.shape[3:]), 2, 0)

    xs = (chunks(q.astype(f32) * DH ** -0.5), chunks(k.astype(f32)), chunks(v.astype(f32)),
          chunks(i_pre.astype(f32)), chunks(jax.nn.log_sigmoid(f_pre.astype(f32))))
    tri = jnp.tril(jnp.ones((L, L), dtype=bool))

    def step(carry, blk):
        C, n, m = carry
        qb, kb, vb, ig, lf = blk
        b = jnp.cumsum(lf, axis=-1)
        d_log = jnp.where(tri, b[..., :, None] - b[..., None, :] + ig[..., None, :], -jnp.inf)
        m_inter = b + m[..., None]
        m_t = jnp.maximum(jnp.max(d_log, axis=-1), m_inter)
        s = jnp.einsum('bhqd,bhkd->bhqk', qb, kb) * jnp.exp(d_log - m_t[..., None])
        w_inter = jnp.exp(m_inter - m_t)
        num = jnp.einsum('bhqk,bhkd->bhqd', s, vb) + w_inter[..., None] * jnp.einsum('bhqd,bhde->bhqe', qb, C)
        den = jnp.sum(s, axis=-1) + w_inter * jnp.einsum('bhqd,bhd->bhq', qb, n)
        h = num / jnp.maximum(jnp.abs(den), jnp.exp(-m_t))[..., None]
        g = b[..., -1:] - b + ig
        m_new = jnp.maximum(b[..., -1] + m, jnp.max(g, axis=-1))
        w_s = jnp.exp(g - m_new[..., None])
        decay = jnp.exp(b[..., -1] + m - m_new)
        C_new = decay[..., None, None] * C + jnp.einsum('bhl,bhld,bhle->bhde', w_s, kb, vb)
        n_new = decay[..., None] * n + jnp.einsum('bhl,bhld->bhd', w_s, kb)
        return (C_new, n_new, m_new), h

    state, hs = lax.scan(step, state, xs)
    h = jnp.moveaxis(hs, 0, 2).reshape(B, H, T, DH)
    return h.astype(q.dtype), state


def mlstm_final_state(k, v, i_pre, f_pre):
    kf, vf = k.astype(jnp.float32), v.astype(jnp.float32)
    b = jnp.cumsum(jax.nn.log_sigmoid(f_pre.astype(jnp.float32)), axis=-1)
    g = b[..., -1:] - b + i_pre.astype(jnp.float32)
    m = jnp.maximum(b[..., -1], jnp.max(g, axis=-1))
    w = jnp.exp(g - m[..., None])
    return (jnp.einsum('bht,bhtd,bhte->bhde', w, kf, vf), jnp.einsum('bht,bhtd->bhd', w, kf), m)


def mlstm_q_o(p, conv_w, conv_b):
    q = short_conv(p[..., OFF_Q:OFF_K], conv_w[:, :M_WIDTH], conv_b[:M_WIDTH])
    return split_heads(q, M_HEADS), p[..., OFF_O:OFF_G]


def mlstm_kv_gates(p, conv_w, conv_b, b_gates):
    B, T, _ = p.shape
    k = short_conv(p[..., OFF_K:OFF_V], conv_w[:, M_WIDTH:], conv_b[M_WIDTH:])
    v = p[..., OFF_V:OFF_O]
    g = (p[..., OFF_G:OFF_CQ] + b_gates).reshape(B, T, 4, M_HEADS).transpose(2, 0, 3, 1)
    return split_heads(k, M_HEADS), split_heads(v, M_HEADS), g[0], g[1], g[2], g[3]


def mlstm_merge(h_f, h_b, o, norm_w):
    h = layer_norm(h_f + h_b)
    B, H, T, DH = h.shape
    h = h.transpose(0, 2, 1, 3).reshape(B, T, H * DH)
    return h * norm_w * jax.nn.sigmoid(o)


def mla_q(p, q_norm_w, w_uq):
    B, T, _ = p.shape
    cq = rms_norm(p[..., OFF_CQ:OFF_CKV], q_norm_w)
    q = jnp.einsum('btr,rc->btc', cq, w_uq).reshape(B, T, A_HEADS, A_NOPE + A_ROPE).transpose(0, 2, 1, 3)
    return q[..., :A_NOPE], q[..., A_NOPE:]


def mla_kv(p, kv_norm_w, w_ukv):
    B, T, _ = p.shape
    ckv = rms_norm(p[..., OFF_CKV:OFF_KR], kv_norm_w)
    kv = jnp.einsum('btr,rc->btc', ckv, w_ukv).reshape(B, T, A_HEADS, A_NOPE + A_V).transpose(0, 2, 1, 3)
    return kv[..., :A_NOPE], p[..., OFF_KR:IN_COLS], kv[..., A_NOPE:]


def mla_attend(qn, qr, kn, kr, v):
    s = (jnp.einsum('bhqd,bhkd->bhqk', qn, kn, preferred_element_type=jnp.float32)
         + jnp.einsum('bhqr,bkr->bhqk', qr, kr, preferred_element_type=jnp.float32)) * A_SCALE
    p = jax.nn.softmax(s, axis=-1)
    return jnp.einsum('bhqk,bhkd->bhqd', p.astype(v.dtype), v)


def mla_blocked(qn, qr, kn, kr, v):
    B, H, T, _ = qn.shape
    nb = T // Q_BLOCK

    def blocks(a):
        return jnp.moveaxis(a.reshape(B, H, nb, Q_BLOCK, a.shape[-1]), 2, 0)

    out = lax.map(lambda qs: mla_attend(qs[0], qs[1], kn, kr, v), (blocks(qn), blocks(qr)))
    return jnp.moveaxis(out, 0, 2).reshape(B, H, T, -1)


def heads_to_tokens(a):
    B, H, T, D = a.shape
    return a.transpose(0, 2, 1, 3).reshape(B, T, H * D)


def token_mixer(h_x, h_c, cos, sin, w_in, b_gates, conv_w, conv_b, m_norm_w, q_norm_w, kv_norm_w,
                w_uq, w_ukv, w_out, ctx_out):
    p_x = jnp.einsum('btd,dc->btc', h_x, w_in)
    p_c = jnp.einsum('btd,dc->btc', h_c, w_in)
    k_c, v_c, if_c, ff_c, ib_c, fb_c = mlstm_kv_gates(p_c, conv_w, conv_b, b_gates)
    if ctx_out:
        q_c, o_c = mlstm_q_o(p_c, conv_w, conv_b)
        h_cf, st_f = mlstm_chunkwise(q_c, k_c, v_c, if_c, ff_c, zero_state(h_c.shape[0]))
        h_cb, st_b = mlstm_chunkwise(flip_t(q_c), flip_t(k_c), flip_t(v_c), flip_t(ib_c), flip_t(fb_c),
                                     zero_state(h_c.shape[0]))
        m_out_c = mlstm_merge(h_cf, flip_t(h_cb), o_c, m_norm_w)
    else:
        st_f = mlstm_final_state(k_c, v_c, if_c, ff_c)
        st_b = mlstm_final_state(flip_t(k_c), flip_t(v_c), flip_t(ib_c), flip_t(fb_c))
    q_x, o_x = mlstm_q_o(p_x, conv_w, conv_b)
    k_x, v_x, if_x, ff_x, ib_x, fb_x = mlstm_kv_gates(p_x, conv_w, conv_b, b_gates)
    h_xf, _ = mlstm_chunkwise(q_x, k_x, v_x, if_x, ff_x, st_f)
    h_xb, _ = mlstm_chunkwise(flip_t(q_x), flip_t(k_x), flip_t(v_x), flip_t(ib_x), flip_t(fb_x), st_b)
    m_out_x = mlstm_merge(h_xf, flip_t(h_xb), o_x, m_norm_w)
    kn_c, kr_c, va_c = mla_kv(p_c, kv_norm_w, w_ukv)
    qn_x, qr_x = mla_q(p_x, q_norm_w, w_uq)
    kn_x, kr_x, va_x = mla_kv(p_x, kv_norm_w, w_ukv)
    qr_x = apply_rope(qr_x, cos, sin)
    kr_x = apply_rope(kr_x, cos, sin)
    kn_all = jnp.concatenate([kn_c, kn_x], axis=2)
    kr_all = jnp.concatenate([kr_c, kr_x], axis=1)
    va_all = jnp.concatenate([va_c, va_x], axis=2)
    a_out_x = heads_to_tokens(mla_blocked(qn_x, qr_x, kn_all, kr_all, va_all))
    y_x = jnp.einsum('btc,cd->btd', jnp.concatenate([m_out_x, a_out_x], axis=-1), w_out)
    if ctx_out:
        qn_c, qr_c = mla_q(p_c, q_norm_w, w_uq)
        a_out_c = heads_to_tokens(mla_attend(qn_c, qr_c, kn_c, kr_c, va_c))
        y_c = jnp.einsum('btc,cd->btd', jnp.concatenate([m_out_c, a_out_c], axis=-1), w_out)
        return y_x, y_c
    return y_x, None


def ec_moe(h, w_router, w_gate, w_up, w_down):
    B, n, D = h.shape
    cap = EC_CAPACITY * n // N_EXPERTS
    aff = jax.nn.softmax(jnp.einsum('bnd,de->bne', h, w_router, preferred_element_type=jnp.float32), axis=-1)
    gate, idx = lax.top_k(aff.transpose(0, 2, 1), cap)
    flat = idx.reshape(B, N_EXPERTS * cap)
    xin = jnp.take_along_axis(h, flat[..., None], axis=1).reshape(B, N_EXPERTS, cap, D)
    act = jax.nn.silu(jnp.einsum('becd,edf->becf', xin, w_gate)) * jnp.einsum('becd,edf->becf', xin, w_up)
    out = jnp.einsum('becf,efd->becd', act, w_down) * gate[..., None].astype(h.dtype)
    return jax.vmap(lambda o, i: jnp.zeros((n, D), o.dtype).at[i].add(o))(out.reshape(B, N_EXPERTS * cap, D), flat)


def setup_inputs(seed: int = 0) -> dict:
    key = jax.random.key(seed)
    ks = jax.random.split(key, 24)
    f32 = jnp.float32
    L = DEPTH

    def nrm(k, shape, scale):
        return jax.random.normal(k, shape, f32) * scale

    gate_base = jnp.concatenate([jnp.zeros((M_HEADS,), f32), jnp.linspace(3.0, 6.0, M_HEADS, dtype=f32),
                                 jnp.zeros((M_HEADS,), f32), jnp.linspace(3.0, 6.0, M_HEADS, dtype=f32)])
    return {
        'x': nrm(ks[0], (BATCH, SEQ, D_MODEL), 1.0),
        'c': nrm(ks[1], (BATCH, D_MODEL), 1.0),
        'ctx': nrm(ks[2], (BATCH, CTX_LEN, D_MODEL), 1.0),
        'c_ctx': nrm(ks[3], (D_MODEL,), 1.0),
        'w_mod': nrm(ks[4], (L, D_MODEL, 6 * D_MODEL), 0.5 * D_MODEL ** -0.5),
        'b_mod': nrm(ks[5], (L, 6 * D_MODEL), 0.02),
        'w_in': nrm(ks[6], (L, D_MODEL, IN_COLS), D_MODEL ** -0.5),
        'b_gates': gate_base + nrm(ks[7], (L, 4 * M_HEADS), 0.1),
        'conv_w': nrm(ks[8], (L, M_CONV, 2 * M_WIDTH), M_CONV ** -0.5),
        'conv_b': nrm(ks[9], (L, 2 * M_WIDTH), 0.02),
        'm_norm_w': 1.0 + nrm(ks[10], (L, M_WIDTH), 0.1),
        'q_norm_w': 1.0 + nrm(ks[11], (L, Q_LORA), 0.1),
        'kv_norm_w': 1.0 + nrm(ks[12], (L, KV_LORA), 0.1),
        'w_uq': nrm(ks[13], (L, Q_LORA, A_HEADS * (A_NOPE + A_ROPE)), Q_LORA ** -0.5),
        'w_ukv': nrm(ks[14], (L, KV_LORA, A_HEADS * (A_NOPE + A_V)), KV_LORA ** -0.5),
        'w_out': nrm(ks[15], (L, MIX_WIDTH, D_MODEL), BETA * MIX_WIDTH ** -0.5),
        'ln1_g': 1.0 + nrm(ks[16], (L, D_MODEL), 0.1),
        'ln1_b': nrm(ks[17], (L, D_MODEL), 0.02),
        'w_router': nrm(ks[18], (L, D_MODEL, N_EXPERTS), D_MODEL ** -0.5),
        'w_gate': nrm(ks[19], (L, N_EXPERTS, D_MODEL, EXPERT_FF), D_MODEL ** -0.5),
        'w_up': nrm(ks[20], (L, N_EXPERTS, D_MODEL, EXPERT_FF), D_MODEL ** -0.5),
        'w_down': nrm(ks[21], (L, N_EXPERTS, EXPERT_FF, D_MODEL), BETA * EXPERT_FF ** -0.5),
        'ln2_g': 1.0 + nrm(ks[22], (L, D_MODEL), 0.1),
        'ln2_b': nrm(ks[23], (L, D_MODEL), 0.02),
    }


def reference(x, c, ctx, c_ctx, w_mod, b_mod, w_in, b_gates, conv_w, conv_b, m_norm_w, q_norm_w, kv_norm_w,
              w_uq, w_ukv, w_out, ln1_g, ln1_b, w_router, w_gate, w_up, w_down, ln2_g, ln2_b):
    cos, sin = axial_rope(x.shape[1])
    for l in range(DEPTH):
        ctx_out = l < DEPTH - 1
        mx = jnp.split((jnp.einsum('bd,dc->bc', jax.nn.silu(c), w_mod[l]) + b_mod[l])[:, None, :], 6, axis=-1)
        mc = jnp.split((jnp.einsum('d,dc->c', jax.nn.silu(c_ctx), w_mod[l]) + b_mod[l])[None, None, :], 6, axis=-1)
        h_x = modulate(x, mx[0], mx[1])
        h_c = modulate(ctx, mc[0], mc[1])
        y_x, y_c = token_mixer(h_x, h_c, cos, sin, w_in[l], b_gates[l], conv_w[l], conv_b[l], m_norm_w[l],
                               q_norm_w[l], kv_norm_w[l], w_uq[l], w_ukv[l], w_out[l], ctx_out)
        x = post_norm(x, mx[2] * y_x, ln1_g[l], ln1_b[l])
        x = post_norm(x, mx[5] * ec_moe(modulate(x, mx[3], mx[4]), w_router[l], w_gate[l], w_up[l], w_down[l]),
                      ln2_g[l], ln2_b[l])
        if ctx_out:
            ctx = post_norm(ctx, mc[2] * y_c, ln1_g[l], ln1_b[l])
            ctx = post_norm(ctx, mc[5] * ec_moe(modulate(ctx, mc[3], mc[4]), w_router[l], w_gate[l], w_up[l],
                                                w_down[l]), ln2_g[l], ln2_b[l])
    return x
```

```python
import functools
import math

import jax
import jax.numpy as jnp
from jax import lax
from jax.experimental import pallas as pl
from jax.experimental.pallas import tpu as pltpu

F32 = jnp.float32
BF16 = jnp.bfloat16
HIGHEST = lax.Precision.HIGHEST

D_MODEL = 1024
M_HEADS = 4
M_HEAD_DIM = 128
M_WIDTH = 512
M_CONV = 5
A_HEADS = 4
A_NOPE = 128
A_ROPE = 64
A_V = 128
A_WIDTH = 512
Q_LORA = 384
KV_LORA = 256
A_SCALE = (A_NOPE + A_ROPE) ** -0.5
ROPE_BASE = 10000.0
GRID_W = 64
N_EXPERTS = 16
EXPERT_FF = 1024
EC_CAPACITY = 2
DEPTH = 2
ALPHA = (2 * DEPTH) ** 0.25
OFF_Q, OFF_K, OFF_V, OFF_O, OFF_G = 0, 512, 1024, 1536, 2048
OFF_CQ = OFF_G + 16
OFF_CKV = OFF_CQ + Q_LORA
OFF_KR = OFF_CKV + KV_LORA
IN_COLS = OFF_KR + A_ROPE

LANES = 128
SUBLANES = 8
VMEM_LIMIT = 56 * 1024 * 1024

P_QK = 0
P_V = 1024
P_O = 1536
P_CQ = 2048
P_CKV = 2432
P_KRA = 2688
P_KRB = 2816
P_G = 2944
P_COLS = 3072

ROW_TILE = 256
MCHUNK = 256


def _cparams(sem):
    return pltpu.CompilerParams(dimension_semantics=sem, vmem_limit_bytes=VMEM_LIMIT)


def _ln(x, eps):
    mu = jnp.mean(x, axis=-1, keepdims=True)
    xc = x - mu
    var = jnp.mean(xc * xc, axis=-1, keepdims=True)
    return xc * lax.rsqrt(var + eps)


def _silu(x):
    return x * (1.0 / (1.0 + jnp.exp(-x)))


def _mod_kernel(c_ref, w_ref, b_ref, o_ref):
    c = c_ref[...]
    a = _silu(c)
    o_ref[0] = jnp.dot(a, w_ref[0], precision=HIGHEST, preferred_element_type=F32) + b_ref[0]


def _modulation(cvec, w_mod, b_mod):
    nl, d, n6 = w_mod.shape
    r = cvec.shape[0]
    tn = 1536
    return pl.pallas_call(
        _mod_kernel,
        grid=(nl, n6 // tn),
        in_specs=[
            pl.BlockSpec((r, d), lambda l, j: (0, 0)),
            pl.BlockSpec((1, d, tn), lambda l, j: (l, 0, j)),
            pl.BlockSpec((1, 1, tn), lambda l, j: (l, 0, j)),
        ],
        out_specs=pl.BlockSpec((1, r, tn), lambda l, j: (l, 0, j)),
        out_shape=jax.ShapeDtypeStruct((nl, r, n6), F32),
        compiler_params=_cparams(("arbitrary", "arbitrary")),
        name="modulation",
    )(cvec, w_mod, b_mod.reshape(nl, 1, n6))


def _inproj_kernel(x_ref, mod_ref, w_ref, wg_ref, qk_ref, v_ref, o_ref, cq_ref, ckv_ref,
                   kra_ref, krb_ref, gcol_ref, grow_ref, *, n_xblk):
    i = pl.program_id(1)
    is_ctx = i >= n_xblk
    x = x_ref[0]
    shift = jnp.where(is_ctx, mod_ref[0, 1, 0:1, :], mod_ref[0, 0, 0:1, :])
    scale = jnp.where(is_ctx, mod_ref[0, 1, 1:2, :], mod_ref[0, 0, 1:2, :])
    h = (_ln(x, 1e-6) * (1.0 + scale) + shift).astype(BF16)
    p = jnp.dot(h, w_ref[...], preferred_element_type=F32)
    qk_ref[0] = p[:, P_QK:P_V]
    v_ref[0] = p[:, P_V:P_O].astype(BF16)
    o_ref[0] = p[:, P_O:P_CQ]
    cq_ref[0] = p[:, P_CQ:P_CKV]
    ckv_ref[0] = p[:, P_CKV:P_KRA]
    kra_ref[0] = p[:, P_KRA:P_KRB]
    krb_ref[0] = p[:, P_KRB:P_G]
    gcol_ref[0] = p[:, P_G:P_COLS]
    grow_ref[0] = lax.dot_general(wg_ref[...], h, (((1,), (1,)), ((), ())), preferred_element_type=F32)


def _inproj(xs, modtab, w_p, w_g, seq):
    b, s, d = xs.shape
    tm = ROW_TILE
    widths = [(1024, F32), (512, BF16), (512, F32), (Q_LORA, F32), (KV_LORA, F32), (128, F32), (128, F32),
              (128, F32)]
    out_shape = [jax.ShapeDtypeStruct((b, s, w), dt) for w, dt in widths]
    out_specs = [pl.BlockSpec((1, tm, w), lambda bi, i: (bi, i, 0)) for w, _ in widths]
    out_shape.append(jax.ShapeDtypeStruct((b, 16, s), F32))
    out_specs.append(pl.BlockSpec((1, 16, tm), lambda bi, i: (bi, 0, i)))
    return pl.pallas_call(
        functools.partial(_inproj_kernel, n_xblk=seq // tm),
        grid=(b, s // tm),
        in_specs=[
            pl.BlockSpec((1, tm, d), lambda bi, i: (bi, i, 0)),
            pl.BlockSpec((1, 2, 6, d), lambda bi, i: (bi, 0, 0, 0)),
            pl.BlockSpec((d, P_COLS), lambda bi, i: (0, 0)),
            pl.BlockSpec((16, d), lambda bi, i: (0, 0)),
        ],
        out_specs=out_specs,
        out_shape=out_shape,
        compiler_params=_cparams(("arbitrary", "arbitrary")),
        name="ln_inproj",
    )(xs, modtab, w_p, w_g)


def _conv_kernel(main_ref, prev_ref, next_ref, w_ref, b_ref, q_ref, kt_ref, ext_ref, *, tm, n_xblk, n_blk):
    i = pl.program_id(1)
    c = main_ref.shape[-1]
    first = jnp.logical_or(i == 0, i == n_xblk)
    last = jnp.logical_or(i == n_xblk - 1, i == n_blk - 1)
    zeros8 = jnp.zeros((SUBLANES, c), F32)
    ext_ref[0:SUBLANES, :] = jnp.where(first, zeros8, prev_ref[0])
    ext_ref[SUBLANES:SUBLANES + tm, :] = main_ref[0]
    ext_ref[SUBLANES + tm:, :] = jnp.where(last, zeros8, next_ref[0])
    acc = jnp.zeros((tm, c), F32) + b_ref[...]
    for j in range(M_CONV):
        off = SUBLANES - M_CONV // 2 + j
        acc = acc + ext_ref[off:off + tm, :] * w_ref[j:j + 1, :]
    y = _silu(acc)
    q_ref[0] = (y[:, :M_WIDTH] * (M_HEAD_DIM ** -0.5)).astype(BF16)
    kt_ref[0] = jnp.transpose(y[:, M_WIDTH:]).astype(BF16)


def _conv(qk_pre, conv_w, conv_b, seq):
    b, s, c = qk_pre.shape
    tm = ROW_TILE
    r8 = tm // SUBLANES
    n_blk = s // tm
    return pl.pallas_call(
        functools.partial(_conv_kernel, tm=tm, n_xblk=seq // tm, n_blk=n_blk),
        grid=(b, n_blk),
        in_specs=[
            pl.BlockSpec((1, tm, c), lambda bi, i: (bi, i, 0)),
            pl.BlockSpec((1, SUBLANES, c), lambda bi, i: (bi, jnp.maximum(i * r8 - 1, 0), 0)),
            pl.BlockSpec((1, SUBLANES, c), lambda bi, i: (bi, jnp.minimum((i + 1) * r8, s // SUBLANES - 1), 0)),
            pl.BlockSpec((SUBLANES, c), lambda bi, i: (0, 0)),
            pl.BlockSpec((1, c), lambda bi, i: (0, 0)),
        ],
        out_specs=[
            pl.BlockSpec((1, tm, M_WIDTH), lambda bi, i: (bi, i, 0)),
            pl.BlockSpec((1, M_WIDTH, tm), lambda bi, i: (bi, 0, i)),
        ],
        out_shape=[
            jax.ShapeDtypeStruct((b, s, M_WIDTH), BF16),
            jax.ShapeDtypeStruct((b, M_WIDTH, s), BF16),
        ],
        scratch_shapes=[pltpu.VMEM((tm + 2 * SUBLANES, c), F32)],
        compiler_params=_cparams(("arbitrary", "arbitrary")),
        name="conv_silu",
    )(qk_pre, qk_pre, qk_pre, conv_w, conv_b)


def _log_sigmoid(x):
    return jnp.minimum(x, 0.0) - jnp.log1p(jnp.exp(-jnp.abs(x)))


def _gate_kernel(gcol_ref, grow_ref, bcol_ref, brow_ref, ocol_ref, orow_ref, *, lc):
    nh = M_HEADS
    r = lax.broadcasted_iota(jnp.int32, (lc, lc), 0)
    c = lax.broadcasted_iota(jnp.int32, (lc, lc), 1)
    lower = (c <= r).astype(F32)
    upper = (c >= r).astype(F32)
    g = gcol_ref[0] + bcol_ref[...]
    lf = _log_sigmoid(g)
    lane = lax.broadcasted_iota(jnp.int32, g.shape, 1)
    lf_f = jnp.where(jnp.logical_and(lane >= nh, lane < 2 * nh), lf, 0.0)
    lf_b = jnp.where(jnp.logical_and(lane >= 3 * nh, lane < 4 * nh), lf, 0.0)
    cs_f = jnp.dot(lower, lf_f, precision=HIGHEST, preferred_element_type=F32)
    cs_b = jnp.dot(upper, lf_b, precision=HIGHEST, preferred_element_type=F32)
    ocol_ref[0] = cs_f + cs_b
    gr = grow_ref[0] + brow_ref[...]
    lfr = _log_sigmoid(gr)
    rs_f = jnp.dot(lfr, upper, precision=HIGHEST, preferred_element_type=F32)
    rs_b = jnp.dot(lfr, lower, precision=HIGHEST, preferred_element_type=F32)
    orow_ref[0, 0:nh, :] = gr[0:nh] - rs_f[nh:2 * nh]
    orow_ref[0, nh:2 * nh, :] = gr[2 * nh:3 * nh] - rs_b[3 * nh:4 * nh]
    orow_ref[0, 2 * nh:3 * nh, :] = rs_f[nh:2 * nh]
    orow_ref[0, 3 * nh:4 * nh, :] = rs_b[3 * nh:4 * nh]


def _gate_prep(gcol, grow, b_gates):
    b, s, _ = gcol.shape
    lc = MCHUNK
    bcol = jnp.zeros((1, LANES), F32).at[0, :16].set(b_gates)
    brow = jnp.broadcast_to(b_gates[:, None], (16, LANES)).astype(F32)
    return pl.pallas_call(
        functools.partial(_gate_kernel, lc=lc),
        grid=(b, s // lc),
        in_specs=[
            pl.BlockSpec((1, lc, LANES), lambda bi, i: (bi, i, 0)),
            pl.BlockSpec((1, 16, lc), lambda bi, i: (bi, 0, i)),
            pl.BlockSpec((1, LANES), lambda bi, i: (0, 0)),
            pl.BlockSpec((16, lc), lambda bi, i: (0, 0)),
        ],
        out_specs=[
            pl.BlockSpec((1, lc, LANES), lambda bi, i: (bi, i, 0)),
            pl.BlockSpec((1, 16, lc), lambda bi, i: (bi, 0, i)),
        ],
        out_shape=[
            jax.ShapeDtypeStruct((b, s, LANES), F32),
            jax.ShapeDtypeStruct((b, 16, s), F32),
        ],
        compiler_params=_cparams(("arbitrary", "arbitrary")),
        name="mlstm_gates",
    )(gcol, grow, bcol, jnp.broadcast_to(b_gates[:, None], (16, lc)).astype(F32))


def _mlstm_kernel(qf_ref, ktf_ref, vf_ref, gcf_ref, grf_ref, qb_ref, ktb_ref, vb_ref, gcb_ref, grb_ref,
                  hf_ref, hb_ref, c_ref, m_ref, *, lc):
    step = pl.program_id(1)
    nh, dh = M_HEADS, M_HEAD_DIM

    @pl.when(step == 0)
    def _():
        c_ref[...] = jnp.zeros(c_ref.shape, F32)
        m_ref[...] = jnp.zeros(m_ref.shape, F32)

    row_i = lax.broadcasted_iota(jnp.int32, (lc, lc), 0)
    col_j = lax.broadcasted_iota(jnp.int32, (lc, lc), 1)
    lane = lax.broadcasted_iota(jnp.int32, (lc, LANES), 1)
    ones_col = jnp.where(lane == 0, 1.0, 0.0).astype(BF16)
    dirs = ((qf_ref, ktf_ref, vf_ref, gcf_ref, grf_ref, hf_ref, col_j <= row_i),
            (qb_ref, ktb_ref, vb_ref, gcb_ref, grb_ref, hb_ref, col_j >= row_i))
    for d, (q_ref, kt_ref, v_ref, gc_ref, gr_ref, h_ref, mask) in enumerate(dirs):
        for hd in range(nh):
            sl = slice(hd * dh, (hd + 1) * dh)
            q = q_ref[0, :, sl]
            kt = kt_ref[0, sl, :]
            v = v_ref[0, :, sl]
            gi = (1 + 2 * d) * nh + hd
            b_col = gc_ref[0, :, gi:gi + 1]
            a_row = gr_ref[0, d * nh + hd:d * nh + hd + 1, :]
            b_row = gr_ref[0, (2 + d) * nh + hd:(2 + d) * nh + hd + 1, :]
            b_last = b_row[:, lc - 1:lc] if d == 0 else b_row[:, 0:1]
            sidx = d * nh + hd
            m_old = m_ref[sidx, :, 0:1]
            c_old = c_ref[sidx]
            d_log = jnp.where(mask, b_col + a_row, -jnp.inf)
            m_inter = b_col + m_old
            m_t = jnp.maximum(jnp.max(d_log, axis=1, keepdims=True), m_inter)
            s_mat = jnp.dot(q, kt, preferred_element_type=F32) * jnp.exp(d_log - m_t)
            w_inter = jnp.exp(m_inter - m_t)
            v_ext = jnp.concatenate([v, ones_col], axis=1)
            num_ext = (jnp.dot(s_mat.astype(BF16), v_ext, preferred_element_type=F32)
                       + w_inter * jnp.dot(q, c_old.astype(BF16), preferred_element_type=F32))
            den = num_ext[:, dh:dh + 1]
            h_ref[0, :, sl] = num_ext[:, :dh] / jnp.maximum(jnp.abs(den), jnp.exp(-m_t))
            g_row = b_last + a_row
            m_new = jnp.maximum(b_last + m_old, jnp.max(g_row, axis=1, keepdims=True))
            w_s = jnp.exp(g_row - m_new)
            decay = jnp.exp(b_last + m_old - m_new)
            kw = (kt.astype(F32) * w_s).astype(BF16)
            c_ref[sidx] = decay * c_old + jnp.dot(kw, v_ext, preferred_element_type=F32)
            m_ref[sidx] = jnp.broadcast_to(m_new, (1, LANES))


def _mlstm(q, kt, v, gcol, grow, seq):
    b, s, w = q.shape
    lc = MCHUNK
    nc, ncx = s // lc, seq // lc
    ncc = nc - ncx

    def fmap(c):
        return jnp.where(c < ncc, ncx + c, c - ncc)

    def bmap(c):
        return jnp.where(c < ncc, nc - 1 - c, ncx - 1 - (c - ncc))

    def specs(cmap):
        return [
            pl.BlockSpec((1, lc, w), lambda bi, c: (bi, cmap(c), 0)),
            pl.BlockSpec((1, w, lc), lambda bi, c: (bi, 0, cmap(c))),
            pl.BlockSpec((1, lc, w), lambda bi, c: (bi, cmap(c), 0)),
            pl.BlockSpec((1, lc, LANES), lambda bi, c: (bi, cmap(c), 0)),
            pl.BlockSpec((1, 16, lc), lambda bi, c: (bi, 0, cmap(c))),
        ]

    return pl.pallas_call(
        functools.partial(_mlstm_kernel, lc=lc),
        grid=(b, nc),
        in_specs=specs(fmap) + specs(bmap),
        out_specs=[
            pl.BlockSpec((1, lc, w), lambda bi, c: (bi, fmap(c), 0)),
            pl.BlockSpec((1, lc, w), lambda bi, c: (bi, bmap(c), 0)),
        ],
        out_shape=[jax.ShapeDtypeStruct((b, s, w), F32), jax.ShapeDtypeStruct((b, s, w), F32)],
        scratch_shapes=[
            pltpu.VMEM((2 * M_HEADS, M_HEAD_DIM, 2 * M_HEAD_DIM), F32),
            pltpu.VMEM((2 * M_HEADS, 1, LANES), F32),
        ],
        compiler_params=_cparams(("arbitrary", "arbitrary")),
        name="mlstm",
    )(q, kt, v, gcol, grow, q, kt, v, gcol, grow)


def _rms(x, g, eps=1e-6):
    return x * lax.rsqrt(jnp.mean(x * x, axis=-1, keepdims=True) + eps) * g


def _mla_prep_kernel(cq_ref, ckv_ref, kra_ref, krb_ref, cos_ref, sin_ref, qg_ref, kvg_ref, wq_ref, wkv_ref,
                     q_ref, k_ref, v_ref):
    nh, dn = A_HEADS, A_NOPE
    cos, sin = cos_ref[...], sin_ref[...]
    cqn = _rms(cq_ref[0], qg_ref[...]).astype(BF16)
    qa = jnp.dot(cqn, wq_ref[...], preferred_element_type=F32)
    ckvn = _rms(ckv_ref[0], kvg_ref[...]).astype(BF16)
    kva = jnp.dot(ckvn, wkv_ref[...], preferred_element_type=F32)
    kr = kra_ref[0] * cos + krb_ref[0] * sin
    for h in range(nh):
        qr = qa[:, (nh + h) * dn:(nh + h + 1) * dn] * cos + qa[:, (2 * nh + h) * dn:(2 * nh + h + 1) * dn] * sin
        q_ref[0, h] = (jnp.concatenate([qa[:, h * dn:(h + 1) * dn], qr], axis=1) * A_SCALE).astype(BF16)
        k_ref[0, h] = jnp.concatenate([kva[:, 2 * h * dn:(2 * h + 1) * dn], kr], axis=1).astype(BF16)
        v_ref[0, h] = kva[:, (2 * h + 1) * dn:(2 * h + 2) * dn].astype(BF16)


def _mla_prep(cq, ckv, kra, krb, cos_t, sin_t, qg, kvg, w_q, w_kv):
    b, s, _ = cq.shape
    tm = ROW_TILE
    nh = A_HEADS
    row = lambda w: pl.BlockSpec((1, tm, w), lambda bi, i: (bi, i, 0))
    const = lambda a: pl.BlockSpec(a.shape, lambda bi, i: (0, 0))
    head = lambda w: pl.BlockSpec((1, nh, tm, w), lambda bi, i: (bi, 0, i, 0))
    return pl.pallas_call(
        _mla_prep_kernel,
        grid=(b, s // tm),
        in_specs=[row(Q_LORA), row(KV_LORA), row(LANES), row(LANES),
                  pl.BlockSpec((tm, LANES), lambda bi, i: (i, 0)), pl.BlockSpec((tm, LANES), lambda bi, i: (i, 0)),
                  const(qg), const(kvg), const(w_q), const(w_kv)],
        out_specs=[head(2 * LANES), head(2 * LANES), head(LANES)],
        out_shape=[jax.ShapeDtypeStruct((b, nh, s, 2 * LANES), BF16),
                   jax.ShapeDtypeStruct((b, nh, s, 2 * LANES), BF16),
                   jax.ShapeDtypeStruct((b, nh, s, LANES), BF16)],
        compiler_params=_cparams(("arbitrary", "arbitrary")),
        name="mla_prep",
    )(cq, ckv, kra, krb, cos_t, sin_t, qg, kvg, w_q, w_kv)


ATT_TQ = 256
ATT_TK = 256


def _attn_kernel(q_ref, k_ref, v_ref, o_ref, *, tq, tk, n_xq, n_kx, nk):
    qi = pl.program_id(2)
    q = q_ref[0, 0]
    lo = jnp.where(qi >= n_xq, n_kx, 0)

    def step(j, carry):
        m, l, acc = carry
        start = pl.multiple_of(j * tk, tk)
        k = k_ref[0, 0, pl.ds(start, tk), :]
        v = v_ref[0, 0, pl.ds(start, tk), :]
        s = lax.dot_general(q, k, (((1,), (1,)), ((), ())), preferred_element_type=F32)
        m_new = jnp.maximum(m, jnp.max(s, axis=1, keepdims=True))
        p = jnp.exp(s - m_new)
        alpha = jnp.exp(m - m_new)
        l = alpha * l + jnp.sum(p, axis=1, keepdims=True)
        acc = alpha * acc + jnp.dot(p.astype(BF16), v, preferred_element_type=F32)
        return m_new, l, acc

    init = (jnp.full((tq, 1), -jnp.inf, F32), jnp.zeros((tq, 1), F32), jnp.zeros((tq, A_V), F32))
    _, l, acc = lax.fori_loop(lo, nk, step, init)
    o_ref[0] = (acc / l).astype(BF16)


def _attention(q, k, v, seq):
    b, nh, s, dq = q.shape
    tq, tk = ATT_TQ, ATT_TK
    return pl.pallas_call(
        functools.partial(_attn_kernel, tq=tq, tk=tk, n_xq=seq // tq, n_kx=seq // tk, nk=s // tk),
        grid=(b, nh, s // tq),
        in_specs=[
            pl.BlockSpec((1, 1, tq, dq), lambda bi, h, i: (bi, h, i, 0)),
            pl.BlockSpec((1, 1, s, dq), lambda bi, h, i: (bi, h, 0, 0)),
            pl.BlockSpec((1, 1, s, A_V), lambda bi, h, i: (bi, h, 0, 0)),
        ],
        out_specs=pl.BlockSpec((1, tq, A_V), lambda bi, h, i: (bi, i, h)),
        out_shape=jax.ShapeDtypeStruct((b, s, nh * A_V), BF16),
        compiler_params=_cparams(("arbitrary", "arbitrary", "arbitrary")),
        name="mla_attention",
    )(q, k, v)


def _outproj_kernel(hf_ref, hb_ref, o_ref, a_ref, x_ref, mod_ref, wout_ref, mn_ref, g_ref, b_ref, wr_ref, wrt_ref,
                    x1_ref, h2_ref, afft_ref, *, n_xblk):
    i = pl.program_id(1)
    is_ctx = i >= n_xblk
    mod = lambda k: jnp.where(is_ctx, mod_ref[0, 1, k:k + 1, :], mod_ref[0, 0, k:k + 1, :])
    hsum = hf_ref[0] + hb_ref[0]
    dh = M_HEAD_DIM
    hn = jnp.concatenate([_ln(hsum[:, h * dh:(h + 1) * dh], 1e-6) for h in range(M_HEADS)], axis=1)
    o = o_ref[0]
    m_out = hn * mn_ref[...] * (1.0 / (1.0 + jnp.exp(-o)))
    cat = jnp.concatenate([m_out.astype(BF16), a_ref[0]], axis=1)
    y = jnp.dot(cat, wout_ref[...], preferred_element_type=F32)
    x1 = _ln(ALPHA * x_ref[0] + mod(2) * y, 1e-5) * g_ref[...] + b_ref[...]
    x1_ref[0] = x1
    h2 = _ln(x1, 1e-6) * (1.0 + mod(4)) + mod(3)
    logits = jnp.dot(h2, wr_ref[...], precision=HIGHEST, preferred_element_type=F32)
    lane = lax.broadcasted_iota(jnp.int32, logits.shape, 1)
    logits = jnp.where(lane < N_EXPERTS, logits, -jnp.inf)
    e = jnp.exp(logits - jnp.max(logits, axis=1, keepdims=True))
    aff = e / jnp.sum(e, axis=1, keepdims=True)
    h2_ref[0] = jnp.concatenate([h2, aff], axis=1)
    lt = lax.dot_general(wrt_ref[...], h2, (((1,), (1,)), ((), ())), precision=HIGHEST,
                         preferred_element_type=F32)
    et = jnp.exp(lt - jnp.max(lt, axis=0, keepdims=True))
    afft_ref[0] = et / jnp.sum(et, axis=0, keepdims=True)


def _outproj(hf, hb, o, a_out, xs, modtab, w_out, mn, g, bb, wr, wrt, seq):
    b, s, d = xs.shape
    tm = ROW_TILE
    row = lambda w: pl.BlockSpec((1, tm, w), lambda bi, i: (bi, i, 0))
    const = lambda a: pl.BlockSpec(a.shape, lambda bi, i: (0, 0))
    return pl.pallas_call(
        functools.partial(_outproj_kernel, n_xblk=seq // tm),
        grid=(b, s // tm),
        in_specs=[row(M_WIDTH), row(M_WIDTH), row(M_WIDTH), row(A_WIDTH), row(d),
                  pl.BlockSpec((1, 2, 6, d), lambda bi, i: (bi, 0, 0, 0)),
                  const(w_out), const(mn), const(g), const(bb), const(wr), const(wrt)],
        out_specs=[row(d), row(d + LANES), pl.BlockSpec((1, N_EXPERTS, tm), lambda bi, i: (bi, 0, i))],
        out_shape=[jax.ShapeDtypeStruct((b, s, d), F32), jax.ShapeDtypeStruct((b, s, d + LANES), F32),
                   jax.ShapeDtypeStruct((b, N_EXPERTS, s), F32)],
        compiler_params=_cparams(("arbitrary", "arbitrary")),
        name="merge_outproj_norm_router",
    )(hf, hb, o, a_out, xs, modtab, w_out, mn, g, bb, wr, wrt)


def _seg(n):
    return -(-n // 1024) * 1024


def _route_kernel(aff_ref, idx_ref, sel_ref, cnt_ref, *, cap, nrow):
    ne = N_EXPERTS
    aff = aff_ref[0]
    bits = pltpu.bitcast(aff, jnp.int32)

    def count(mask):
        return jnp.sum(jnp.sum(mask.astype(F32), axis=2, keepdims=True), axis=1, keepdims=True)

    def bit_step(i, thr):
        cand = jnp.bitwise_or(thr, lax.shift_left(jnp.int32(1), 30 - i))
        return jnp.where(count(bits >= cand) >= cap, cand, thr)

    thr = lax.fori_loop(0, 31, bit_step, jnp.zeros((ne, 1, 1), jnp.int32))
    gt = bits > thr
    eq = bits == thr
    need = cap - count(gt)

    li = lax.broadcasted_iota(jnp.int32, (LANES, LANES), 0)
    lj = lax.broadcasted_iota(jnp.int32, (LANES, LANES), 1)
    tri = (li <= lj).astype(BF16)
    nr = ne * nrow
    shift = int(math.log2(nrow))
    bi = lax.broadcasted_iota(jnp.int32, (nr, nr), 0)
    bj = lax.broadcasted_iota(jnp.int32, (nr, nr), 1)
    same = lax.shift_right_logical(bi, shift) == lax.shift_right_logical(bj, shift)
    blk = jnp.logical_and(same, bj < bi).astype(BF16)

    def prefix(mask):
        m = mask.astype(BF16).reshape(nr, LANES)
        local = jnp.dot(m, tri, preferred_element_type=F32)
        tot = jnp.broadcast_to(local[:, LANES - 1:LANES], (nr, LANES)).astype(BF16)
        offs = jnp.dot(blk, tot, preferred_element_type=F32)
        return local, offs

    eq_local, eq_offs = prefix(eq)
    eq_f = eq.astype(F32)
    eq_rank = (eq_local + eq_offs).reshape(ne, nrow, LANES) - eq_f
    sel = jnp.logical_or(gt, jnp.logical_and(eq, eq_rank < need))
    local, offs = prefix(sel)
    cnt = (local + offs).reshape(ne, nrow, LANES)
    sel_ref[0] = sel.astype(BF16)
    cnt_ref[0] = cnt.astype(jnp.int32)

    rc = jnp.max(cnt, axis=2)
    rtot = jnp.sum(sel.astype(F32), axis=2)
    rc_excl = rc - rtot
    local3 = local.reshape(ne, nrow, LANES)
    p_r = lax.broadcasted_iota(jnp.int32, (cap, nrow), 0).astype(F32)
    r_r = lax.broadcasted_iota(jnp.int32, (cap, nrow), 1).astype(F32)
    p_l = lax.broadcasted_iota(jnp.int32, (cap, LANES), 0).astype(F32)
    lane = lax.broadcasted_iota(jnp.int32, (cap, LANES), 1)
    out = jnp.zeros((cap, LANES), jnp.int32)
    for e in range(ne):
        rowsel = jnp.sum((rc[e:e + 1, :] <= p_r).astype(F32), axis=1, keepdims=True)
        onehot = (r_r == rowsel)
        in_row = jnp.dot(onehot.astype(BF16), local3[e].astype(BF16), preferred_element_type=F32)
        before = jnp.sum(jnp.where(onehot, rc_excl[e:e + 1, :], 0.0), axis=1, keepdims=True)
        lane_of = jnp.sum((in_row + before <= p_l).astype(F32), axis=1, keepdims=True)
        tok = (rowsel * LANES + lane_of).astype(jnp.int32)
        out = jnp.where(lane == e, tok, out)
    idx_ref[0] = out


def _route(aff4, cap):
    b, ne, nrow, _ = aff4.shape
    return pl.pallas_call(
        functools.partial(_route_kernel, cap=cap, nrow=nrow),
        grid=(b,),
        in_specs=[pl.BlockSpec((1, ne, nrow, LANES), lambda bi: (bi, 0, 0, 0))],
        out_specs=[pl.BlockSpec((1, cap, LANES), lambda bi: (bi, 0, 0)),
                   pl.BlockSpec((1, ne, nrow, LANES), lambda bi: (bi, 0, 0, 0)),
                   pl.BlockSpec((1, ne, nrow, LANES), lambda bi: (bi, 0, 0, 0))],
        out_shape=[jax.ShapeDtypeStruct((b, cap, LANES), jnp.int32),
                   jax.ShapeDtypeStruct((b, ne, nrow, LANES), BF16),
                   jax.ShapeDtypeStruct((b, ne, nrow, LANES), jnp.int32)],
        compiler_params=_cparams(("arbitrary",)),
        name="ec_route",
    )(aff4)


def _expert_kernel(idx_hbm, h_hbm, wg_ref, wu_ref, wd_ref, y_ref, idx_smem, xin_ref, sem_i, sem_g, *,
                   cap, s_rows, row_off):
    e = pl.program_id(0)
    b = pl.program_id(1)
    d = D_MODEL
    seg = idx_smem.shape[0]
    icp = pltpu.make_async_copy(idx_hbm.at[pl.ds(pl.multiple_of((b * N_EXPERTS + e) * seg, seg), seg)], idx_smem,
                                sem_i)
    icp.start()
    icp.wait()
    base = b * s_rows + row_off

    def row_copy(p):
        return pltpu.make_async_copy(h_hbm.at[pl.ds(base + idx_smem[p], 1)], xin_ref.at[pl.ds(p, 1)], sem_g)

    def issue(p, c):
        row_copy(p).start()
        return c

    def drain(p, c):
        row_copy(p).wait()
        return c

    lax.fori_loop(0, cap, issue, 0)
    lax.fori_loop(0, cap, drain, 0)
    x = xin_ref[:, :d].astype(BF16)
    aff = xin_ref[:, d:]
    lane = lax.broadcasted_iota(jnp.int32, aff.shape, 1)
    gate = jnp.sum(jnp.where(lane == e, aff, 0.0), axis=1, keepdims=True)
    hg = jnp.dot(x, wg_ref[0], preferred_element_type=F32)
    hu = jnp.dot(x, wu_ref[0], preferred_element_type=F32)
    act = (_silu(hg) * hu).astype(BF16)
    y_ref[0] = jnp.dot(act, wd_ref[0], preferred_element_type=F32) * gate


def _experts(idx_tab, h2a, w_gate, w_up, w_down, cap, row_off):
    b, s, dw = h2a.shape
    ne, d, ff = w_gate.shape
    wspec = lambda r, c: pl.BlockSpec((1, r, c), lambda e, bi: (e, 0, 0))
    return pl.pallas_call(
        functools.partial(_expert_kernel, cap=cap, s_rows=s, row_off=row_off),
        grid=(ne, b),
        in_specs=[pl.BlockSpec(memory_space=pl.ANY), pl.BlockSpec(memory_space=pl.ANY),
                  wspec(d, ff), wspec(d, ff), wspec(ff, d)],
        out_specs=pl.BlockSpec((1, cap, d), lambda e, bi: (bi, e, 0)),
        out_shape=jax.ShapeDtypeStruct((b, ne * cap, d), F32),
        scratch_shapes=[pltpu.SMEM((_seg(cap),), jnp.int32), pltpu.VMEM((cap, dw), F32),
                        pltpu.SemaphoreType.DMA(()), pltpu.SemaphoreType.DMA(())],
        compiler_params=_cparams(("arbitrary", "arbitrary")),
        name="ec_experts",
    )(idx_tab, h2a.reshape(b * s, dw), w_gate, w_up, w_down)


COMBINE_TB = 128


def _combine_kernel(idx_hbm, bnd_hbm, y_hbm, sel_ref, x1_ref, mod_ref, g_ref, b_ref, *rest, cap, tb, is_ctx, nblk,
                    has_tail):
    if has_tail:
        tail_ref, o_ref, idx_smem, bnd_smem, buf_ref, sem_i, sem_g = rest
    else:
        o_ref, idx_smem, bnd_smem, buf_ref, sem_i, sem_g = rest
    b = pl.program_id(0)
    j = pl.program_id(1)
    ne = N_EXPERTS

    @pl.when(jnp.logical_and(b == 0, j == 0))
    def _():
        buf_ref[...] = jnp.zeros(buf_ref.shape, F32)

    @pl.when(j == 0)
    def _():
        n_i, n_b = idx_smem.shape[0], bnd_smem.shape[0]
        c1 = pltpu.make_async_copy(idx_hbm.at[pl.ds(pl.multiple_of(b * n_i, n_i), n_i)], idx_smem, sem_i)
        c2 = pltpu.make_async_copy(bnd_hbm.at[pl.ds(pl.multiple_of(b * n_b, n_b), n_b)], bnd_smem, sem_i)
        c1.start()
        c2.start()
        c1.wait()
        c2.wait()

    @pl.when(j < nblk)
    def _():
        ybase = b * ne * cap

        def row_copy(e, p):
            tl = idx_smem[e * _seg(cap) + p] - j * tb
            return pltpu.make_async_copy(y_hbm.at[pl.ds(ybase + e * cap + p, 1)], buf_ref.at[e, pl.ds(tl, 1)], sem_g)

        def issue(e, p, c):
            row_copy(e, p).start()
            return c

        def drain(e, p, c):
            row_copy(e, p).wait()
            return c

        for e in range(ne):
            lo = bnd_smem[e * (nblk + 1) + j]
            hi = bnd_smem[e * (nblk + 1) + j + 1]
            lax.fori_loop(lo, hi, functools.partial(issue, e), 0)
        for e in range(ne):
            lo = bnd_smem[e * (nblk + 1) + j]
            hi = bnd_smem[e * (nblk + 1) + j + 1]
            lax.fori_loop(lo, hi, functools.partial(drain, e), 0)

        ri = lax.broadcasted_iota(jnp.int32, (tb, tb), 0)
        ci = lax.broadcasted_iota(jnp.int32, (tb, tb), 1)
        eye = (ri == ci).astype(BF16)
        mask_t = lax.dot_general(eye, sel_ref[0], (((1,), (1,)), ((), ())), preferred_element_type=F32)
        moe = jnp.zeros((tb, D_MODEL), F32)
        for e in range(ne):
            moe = moe + jnp.where(mask_t[:, e:e + 1] > 0.5, buf_ref[e], 0.0)
        k = 1 if is_ctx else 0
        o_ref[0] = _ln(ALPHA * x1_ref[0] + mod_ref[0, k, 5:6, :] * moe, 1e-5) * g_ref[...] + b_ref[...]

    if has_tail:
        @pl.when(j >= nblk)
        def _():
            o_ref[0] = tail_ref[0]


def _combine(idx_flat, bnd_flat, y_all, sel, x1, modtab, g, bb, cap, is_ctx, blk_off, tail=None):
    b, ne, n = sel.shape
    d = x1.shape[-1]
    tb = COMBINE_TB
    nblk = n // tb
    ntail = 0 if tail is None else tail.shape[1] // tb
    last = nblk - 1
    in_specs = [pl.BlockSpec(memory_space=pl.ANY), pl.BlockSpec(memory_space=pl.ANY), pl.BlockSpec(memory_space=pl.ANY),
                pl.BlockSpec((1, ne, tb), lambda bi, j: (bi, 0, jnp.minimum(j, last))),
                pl.BlockSpec((1, tb, d), lambda bi, j: (bi, jnp.minimum(j, last) + blk_off, 0)),
                pl.BlockSpec((1, 2, 6, d), lambda bi, j: (bi, 0, 0, 0)),
                pl.BlockSpec((1, d), lambda bi, j: (0, 0)), pl.BlockSpec((1, d), lambda bi, j: (0, 0))]
    args = [idx_flat, bnd_flat, y_all.reshape(b * ne * cap, d), sel, x1, modtab, g, bb]
    if tail is not None:
        in_specs.append(pl.BlockSpec((1, tb, d), lambda bi, j: (bi, jnp.maximum(j - nblk, 0), 0)))
        args.append(tail)
    return pl.pallas_call(
        functools.partial(_combine_kernel, cap=cap, tb=tb, is_ctx=is_ctx, nblk=nblk, has_tail=tail is not None),
        grid=(b, nblk + ntail),
        in_specs=in_specs,
        out_specs=pl.BlockSpec((1, tb, d), lambda bi, j: (bi, j, 0)),
        out_shape=jax.ShapeDtypeStruct((b, (nblk + ntail) * tb, d), F32),
        scratch_shapes=[pltpu.SMEM((ne * _seg(cap),), jnp.int32), pltpu.SMEM((_seg(ne * (nblk + 1)),), jnp.int32),
                        pltpu.VMEM((ne, tb, d), F32), pltpu.SemaphoreType.DMA(()), pltpu.SemaphoreType.DMA(())],
        compiler_params=_cparams(("arbitrary", "arbitrary")),
        name="ec_combine_norm",
    )(*args)


def _moe_set(afft_set, h2a, x1, modtab, lw, cap, row_off, is_ctx, tail=None):
    b, ne, n = afft_set.shape
    n_pad = max(n, 16 * LANES)
    aff4 = jnp.pad(afft_set, ((0, 0), (0, 0), (0, n_pad - n)), constant_values=-1.0).reshape(b, ne, n_pad // LANES, LANES)
    idx, sel, cnt = _route(aff4, cap)
    idx_t = jnp.transpose(idx[:, :, :ne], (0, 2, 1))
    idx_flat = jnp.pad(idx_t, ((0, 0), (0, 0), (0, _seg(cap) - cap))).reshape(-1)
    nblk = n // COMBINE_TB
    ends = cnt[:, :, :nblk, LANES - 1]
    bnd = jnp.concatenate([jnp.zeros((b, ne, 1), jnp.int32), ends], axis=2).reshape(b, -1)
    bnd_flat = jnp.pad(bnd, ((0, 0), (0, _seg(bnd.shape[1]) - bnd.shape[1]))).reshape(-1)
    y_all = _experts(idx_flat, h2a, lw["w_gate"], lw["w_up"], lw["w_down"], cap, row_off)
    sel_rows = sel.reshape(b, ne, n_pad)[:, :, :n]
    return _combine(idx_flat, bnd_flat, y_all, sel_rows, x1, modtab, lw["ln2_g"], lw["ln2_b"], cap, is_ctx,
                    row_off // COMBINE_TB, tail)


def _rope_tables(seq, ctx_len):
    rows = seq // GRID_W
    half = A_ROPE // 2
    inv = ROPE_BASE ** (-jnp.arange(0, half, 2, dtype=F32) / half)
    row = jnp.repeat(jnp.arange(rows, dtype=F32), GRID_W)
    col = jnp.tile(jnp.arange(GRID_W, dtype=F32), rows)
    ang = jnp.concatenate([row[:, None] * inv, col[:, None] * inv], axis=-1)
    cos, sin = jnp.cos(ang), jnp.sin(ang)
    pad = jnp.zeros((seq, LANES - A_ROPE), F32)
    cos_x = jnp.concatenate([cos, cos, pad], axis=1)
    sin_x = jnp.concatenate([sin, sin, pad], axis=1)
    cos_c = jnp.concatenate([jnp.ones((ctx_len, A_ROPE), F32), jnp.zeros((ctx_len, LANES - A_ROPE), F32)], axis=1)
    sin_c = jnp.zeros((ctx_len, LANES), F32)
    return jnp.concatenate([cos_x, cos_c], axis=0), jnp.concatenate([sin_x, sin_c], axis=0)


def _rot(w):
    half = A_ROPE // 2
    return jnp.concatenate([-w[..., half:], w[..., :half]], axis=-1)


def _prepare(p):
    c, c_ctx = p["c"], p["c_ctx"]
    b, d = c.shape
    nl = p["w_mod"].shape[0]
    rows = -(-(b + 1) // SUBLANES) * SUBLANES
    cvec = jnp.zeros((rows, d), F32).at[:b].set(c).at[b].set(c_ctx)
    mods = _modulation(cvec, p["w_mod"], p["b_mod"]).reshape(nl, rows, 6, d)
    mx = mods[:, :b]
    mc = jnp.broadcast_to(mods[:, b:b + 1], (nl, b, 6, d))
    modtab = jnp.stack([mx, mc], axis=2)
    layers = []
    for l in range(nl):
        w_in = p["w_in"][l]
        w_kr = w_in[:, OFF_KR:IN_COLS]
        z64 = jnp.zeros((d, LANES - A_ROPE), F32)
        w_p = jnp.concatenate([
            w_in[:, OFF_Q:OFF_G], w_in[:, OFF_CQ:OFF_KR], w_kr, z64, _rot(w_kr), z64,
            w_in[:, OFF_G:OFF_CQ], jnp.zeros((d, LANES - 16), F32)], axis=1).astype(BF16)
        w_uq = p["w_uq"][l].reshape(Q_LORA, A_HEADS, A_NOPE + A_ROPE)
        zq = jnp.zeros((Q_LORA, A_HEADS, LANES - A_ROPE), F32)
        w_q = jnp.concatenate([
            w_uq[:, :, :A_NOPE].reshape(Q_LORA, -1),
            jnp.concatenate([w_uq[:, :, A_NOPE:], zq], axis=2).reshape(Q_LORA, -1),
            jnp.concatenate([_rot(w_uq[:, :, A_NOPE:]), zq], axis=2).reshape(Q_LORA, -1)], axis=1).astype(BF16)
        w_router = p["w_router"][l]
        layers.append(dict(
            w_p=w_p,
            w_g=jnp.transpose(w_in[:, OFF_G:OFF_CQ]).astype(BF16),
            b_gates=p["b_gates"][l],
            conv_w=jnp.zeros((SUBLANES, 2 * M_WIDTH), F32).at[:M_CONV].set(p["conv_w"][l]),
            conv_b=p["conv_b"][l].reshape(1, -1),
            w_q=w_q,
            w_kv=p["w_ukv"][l].astype(BF16),
            qg=p["q_norm_w"][l].reshape(1, -1),
            kvg=p["kv_norm_w"][l].reshape(1, -1),
            w_out=p["w_out"][l].astype(BF16),
            mn=p["m_norm_w"][l].reshape(1, -1),
            ln1_g=p["ln1_g"][l].reshape(1, -1),
            ln1_b=p["ln1_b"][l].reshape(1, -1),
            wr=jnp.concatenate([w_router, jnp.zeros((d, LANES - N_EXPERTS), F32)], axis=1),
            wrt=jnp.transpose(w_router),
            w_gate=p["w_gate"][l].astype(BF16),
            w_up=p["w_up"][l].astype(BF16),
            w_down=p["w_down"][l].astype(BF16),
            ln2_g=p["ln2_g"][l].reshape(1, -1),
            ln2_b=p["ln2_b"][l].reshape(1, -1),
        ))
    cos_t, sin_t = _rope_tables(p["x"].shape[1], p["ctx"].shape[1])
    return dict(modtab=modtab, layers=layers, cos=cos_t, sin=sin_t)


def _layer(xs, st, l, seq, last):
    b, s, d = xs.shape
    lw, modtab = st["layers"][l], st["modtab"][l]
    qk_pre, v, o, cq, ckv, kra, krb, gcol, grow = _inproj(xs, modtab, lw["w_p"], lw["w_g"], seq)
    q, kt = _conv(qk_pre, lw["conv_w"], lw["conv_b"], seq)
    gc, gr = _gate_prep(gcol, grow, lw["b_gates"])
    hf, hb = _mlstm(q, kt, v, gc, gr, seq)
    qa, ka, va = _mla_prep(cq, ckv, kra, krb, st["cos"], st["sin"], lw["qg"], lw["kvg"], lw["w_q"], lw["w_kv"])
    a_out = _attention(qa, ka, va, seq)
    x1, h2a, afft = _outproj(hf, hb, o, a_out, xs, modtab, lw["w_out"], lw["mn"], lw["ln1_g"], lw["ln1_b"],
                             lw["wr"], lw["wrt"], seq)
    cap_x = EC_CAPACITY * seq // N_EXPERTS
    if last:
        return _moe_set(afft[:, :, :seq], h2a, x1, modtab, lw, cap_x, 0, False)
    cap_c = EC_CAPACITY * (s - seq) // N_EXPERTS
    x2_ctx = _moe_set(afft[:, :, seq:], h2a, x1, modtab, lw, cap_c, seq, True)
    return _moe_set(afft[:, :, :seq], h2a, x1, modtab, lw, cap_x, 0, False, tail=x2_ctx)


def kernel(x, c, ctx, c_ctx, w_mod, b_mod, w_in, b_gates, conv_w, conv_b, m_norm_w, q_norm_w, kv_norm_w, w_uq, w_ukv,
           w_out, ln1_g, ln1_b, w_router, w_gate, w_up, w_down, ln2_g, ln2_b):
    p = dict(x=x, c=c, ctx=ctx, c_ctx=c_ctx, w_mod=w_mod, b_mod=b_mod, w_in=w_in, b_gates=b_gates, conv_w=conv_w,
             conv_b=conv_b, m_norm_w=m_norm_w, q_norm_w=q_norm_w, kv_norm_w=kv_norm_w, w_uq=w_uq, w_ukv=w_ukv,
             w_out=w_out, ln1_g=ln1_g, ln1_b=ln1_b, w_router=w_router, w_gate=w_gate, w_up=w_up, w_down=w_down,
             ln2_g=ln2_g, ln2_b=ln2_b)
    st = _prepare(p)
    seq = x.shape[1]
    xs = jnp.concatenate([x, ctx], axis=1)
    nl = w_in.shape[0]
    for l in range(nl):
        xs = _layer(xs, st, l, seq, last=(l == nl - 1))
    return xs
```

```python
import functools
import math

import jax
import jax.numpy as jnp
from jax import lax
from jax.experimental import pallas as pl
from jax.experimental.pallas import tpu as pltpu

F32 = jnp.float32
BF16 = jnp.bfloat16
HIGHEST = lax.Precision.HIGHEST

D_MODEL = 1024
M_HEADS = 4
M_HEAD_DIM = 128
M_WIDTH = 512
M_CONV = 5
A_HEADS = 4
A_NOPE = 128
A_ROPE = 64
A_V = 128
A_WIDTH = 512
Q_LORA = 384
KV_LORA = 256
A_SCALE = (A_NOPE + A_ROPE) ** -0.5
Q_SCALE = A_SCALE * math.log2(math.e)
ROPE_BASE = 10000.0
GRID_W = 64
N_EXPERTS = 16
EXPERT_FF = 1024
EC_CAPACITY = 2
DEPTH = 2
ALPHA = (2 * DEPTH) ** 0.25
OFF_Q, OFF_K, OFF_V, OFF_O, OFF_G = 0, 512, 1024, 1536, 2048
OFF_CQ = OFF_G + 16
OFF_CKV = OFF_CQ + Q_LORA
OFF_KR = OFF_CKV + KV_LORA
IN_COLS = OFF_KR + A_ROPE

LANES = 128
SUBLANES = 8
VMEM_LIMIT = 56 * 1024 * 1024

P_QK = 0
P_V = 1024
P_O = 1536
P_CQ = 2048
P_CKV = 2432
P_KRA = 2688
P_KRB = 2816
P_G = 2944
P_COLS = 3072

ROW_TILE = 256
MCHUNK = 256


def _cparams(sem):
    return pltpu.CompilerParams(dimension_semantics=sem, vmem_limit_bytes=VMEM_LIMIT)


def _ln(x, eps):
    mu = jnp.mean(x, axis=-1, keepdims=True)
    xc = x - mu
    var = jnp.mean(xc * xc, axis=-1, keepdims=True)
    return xc * lax.rsqrt(var + eps)


def _silu(x):
    return x * (1.0 / (1.0 + jnp.exp(-x)))


def _mod_kernel(c_ref, w_ref, b_ref, o_ref):
    c = c_ref[...]
    a = _silu(c)
    o_ref[0] = jnp.dot(a, w_ref[0], precision=HIGHEST, preferred_element_type=F32) + b_ref[0]


def _modulation(cvec, w_mod, b_mod):
    nl, d, n6 = w_mod.shape
    r = cvec.shape[0]
    tn = 1536
    return pl.pallas_call(
        _mod_kernel,
        grid=(nl, n6 // tn),
        in_specs=[
            pl.BlockSpec((r, d), lambda l, j: (0, 0)),
            pl.BlockSpec((1, d, tn), lambda l, j: (l, 0, j)),
            pl.BlockSpec((1, 1, tn), lambda l, j: (l, 0, j)),
        ],
        out_specs=pl.BlockSpec((1, r, tn), lambda l, j: (l, 0, j)),
        out_shape=jax.ShapeDtypeStruct((nl, r, n6), F32),
        compiler_params=_cparams(("arbitrary", "arbitrary")),
        name="modulation",
    )(cvec, w_mod, b_mod.reshape(nl, 1, n6))


def _inproj_kernel(x_ref, mod_ref, w_ref, wg_ref, qk_ref, v_ref, o_ref, cq_ref, ckv_ref,
                   kra_ref, krb_ref, gcol_ref, grow_ref, *, n_xblk):
    i = pl.program_id(1)
    is_ctx = i >= n_xblk
    x = x_ref[0]
    shift = jnp.where(is_ctx, mod_ref[0, 1, 0:1, :], mod_ref[0, 0, 0:1, :])
    scale = jnp.where(is_ctx, mod_ref[0, 1, 1:2, :], mod_ref[0, 0, 1:2, :])
    h = (_ln(x, 1e-6) * (1.0 + scale) + shift).astype(BF16)
    p = jnp.dot(h, w_ref[...], preferred_element_type=F32)
    qk_ref[0] = p[:, P_QK:P_V]
    v_ref[0] = p[:, P_V:P_O].astype(BF16)
    o_ref[0] = p[:, P_O:P_CQ]
    cq_ref[0] = p[:, P_CQ:P_CKV]
    ckv_ref[0] = p[:, P_CKV:P_KRA]
    kra_ref[0] = p[:, P_KRA:P_KRB]
    krb_ref[0] = p[:, P_KRB:P_G]
    gcol_ref[0] = p[:, P_G:P_COLS]
    grow_ref[0] = lax.dot_general(wg_ref[...], h, (((1,), (1,)), ((), ())), preferred_element_type=F32)


def _inproj(xs, modtab, w_p, w_g, seq):
    b, s, d = xs.shape
    tm = ROW_TILE
    widths = [(1024, F32), (512, BF16), (512, F32), (Q_LORA, F32), (KV_LORA, F32), (128, F32), (128, F32),
              (128, F32)]
    out_shape = [jax.ShapeDtypeStruct((b, s, w), dt) for w, dt in widths]
    out_specs = [pl.BlockSpec((1, tm, w), lambda bi, i: (bi, i, 0)) for w, _ in widths]
    out_shape.append(jax.ShapeDtypeStruct((b, 16, s), F32))
    out_specs.append(pl.BlockSpec((1, 16, tm), lambda bi, i: (bi, 0, i)))
    return pl.pallas_call(
        functools.partial(_inproj_kernel, n_xblk=seq // tm),
        grid=(b, s // tm),
        in_specs=[
            pl.BlockSpec((1, tm, d), lambda bi, i: (bi, i, 0)),
            pl.BlockSpec((1, 2, 6, d), lambda bi, i: (bi, 0, 0, 0)),
            pl.BlockSpec((d, P_COLS), lambda bi, i: (0, 0)),
            pl.BlockSpec((16, d), lambda bi, i: (0, 0)),
        ],
        out_specs=out_specs,
        out_shape=out_shape,
        compiler_params=_cparams(("arbitrary", "arbitrary")),
        name="ln_inproj",
    )(xs, modtab, w_p, w_g)


def _conv_kernel(main_ref, prev_ref, next_ref, w_ref, b_ref, q_ref, kt_ref, ext_ref, *, tm, n_xblk, n_blk):
    i = pl.program_id(1)
    c = main_ref.shape[-1]
    first = jnp.logical_or(i == 0, i == n_xblk)
    last = jnp.logical_or(i == n_xblk - 1, i == n_blk - 1)
    zeros8 = jnp.zeros((SUBLANES, c), F32)
    ext_ref[0:SUBLANES, :] = jnp.where(first, zeros8, prev_ref[0])
    ext_ref[SUBLANES:SUBLANES + tm, :] = main_ref[0]
    ext_ref[SUBLANES + tm:, :] = jnp.where(last, zeros8, next_ref[0])
    acc = jnp.zeros((tm, c), F32) + b_ref[...]
    for j in range(M_CONV):
        off = SUBLANES - M_CONV // 2 + j
        acc = acc + ext_ref[off:off + tm, :] * w_ref[j:j + 1, :]
    y = _silu(acc)
    q_ref[0] = (y[:, :M_WIDTH] * (M_HEAD_DIM ** -0.5)).astype(BF16)
    kt_ref[0] = jnp.transpose(y[:, M_WIDTH:]).astype(BF16)


def _conv(qk_pre, conv_w, conv_b, seq):
    b, s, c = qk_pre.shape
    tm = ROW_TILE
    r8 = tm // SUBLANES
    n_blk = s // tm
    return pl.pallas_call(
        functools.partial(_conv_kernel, tm=tm, n_xblk=seq // tm, n_blk=n_blk),
        grid=(b, n_blk),
        in_specs=[
            pl.BlockSpec((1, tm, c), lambda bi, i: (bi, i, 0)),
            pl.BlockSpec((1, SUBLANES, c), lambda bi, i: (bi, jnp.maximum(i * r8 - 1, 0), 0)),
            pl.BlockSpec((1, SUBLANES, c), lambda bi, i: (bi, jnp.minimum((i + 1) * r8, s // SUBLANES - 1), 0)),
            pl.BlockSpec((SUBLANES, c), lambda bi, i: (0, 0)),
            pl.BlockSpec((1, c), lambda bi, i: (0, 0)),
        ],
        out_specs=[
            pl.BlockSpec((1, tm, M_WIDTH), lambda bi, i: (bi, i, 0)),
            pl.BlockSpec((1, M_WIDTH, tm), lambda bi, i: (bi, 0, i)),
        ],
        out_shape=[
            jax.ShapeDtypeStruct((b, s, M_WIDTH), BF16),
            jax.ShapeDtypeStruct((b, M_WIDTH, s), BF16),
        ],
        scratch_shapes=[pltpu.VMEM((tm + 2 * SUBLANES, c), F32)],
        compiler_params=_cparams(("arbitrary", "arbitrary")),
        name="conv_silu",
    )(qk_pre, qk_pre, qk_pre, conv_w, conv_b)


def _log_sigmoid(x):
    return jnp.minimum(x, 0.0) - jnp.log1p(jnp.exp(-jnp.abs(x)))


def _gate_kernel(gcol_ref, grow_ref, bcol_ref, brow_ref, ocol_ref, orow_ref, *, lc):
    nh = M_HEADS
    r = lax.broadcasted_iota(jnp.int32, (lc, lc), 0)
    c = lax.broadcasted_iota(jnp.int32, (lc, lc), 1)
    lower = (c <= r).astype(F32)
    upper = (c >= r).astype(F32)
    g = gcol_ref[0] + bcol_ref[...]
    lf = _log_sigmoid(g)
    lane = lax.broadcasted_iota(jnp.int32, g.shape, 1)
    lf_f = jnp.where(jnp.logical_and(lane >= nh, lane < 2 * nh), lf, 0.0)
    lf_b = jnp.where(jnp.logical_and(lane >= 3 * nh, lane < 4 * nh), lf, 0.0)
    cs_f = jnp.dot(lower, lf_f, precision=HIGHEST, preferred_element_type=F32)
    cs_b = jnp.dot(upper, lf_b, precision=HIGHEST, preferred_element_type=F32)
    ocol_ref[0] = cs_f + cs_b
    gr = grow_ref[0] + brow_ref[...]
    lfr = _log_sigmoid(gr)
    rs_f = jnp.dot(lfr, upper, precision=HIGHEST, preferred_element_type=F32)
    rs_b = jnp.dot(lfr, lower, precision=HIGHEST, preferred_element_type=F32)
    orow_ref[0, 0:nh, :] = gr[0:nh] - rs_f[nh:2 * nh]
    orow_ref[0, nh:2 * nh, :] = gr[2 * nh:3 * nh] - rs_b[3 * nh:4 * nh]
    orow_ref[0, 2 * nh:3 * nh, :] = rs_f[nh:2 * nh]
    orow_ref[0, 3 * nh:4 * nh, :] = rs_b[3 * nh:4 * nh]


def _gate_prep(gcol, grow, b_gates):
    b, s, _ = gcol.shape
    lc = MCHUNK
    bcol = jnp.zeros((1, LANES), F32).at[0, :16].set(b_gates)
    brow = jnp.broadcast_to(b_gates[:, None], (16, LANES)).astype(F32)
    return pl.pallas_call(
        functools.partial(_gate_kernel, lc=lc),
        grid=(b, s // lc),
        in_specs=[
            pl.BlockSpec((1, lc, LANES), lambda bi, i: (bi, i, 0)),
            pl.BlockSpec((1, 16, lc), lambda bi, i: (bi, 0, i)),
            pl.BlockSpec((1, LANES), lambda bi, i: (0, 0)),
            pl.BlockSpec((16, lc), lambda bi, i: (0, 0)),
        ],
        out_specs=[
            pl.BlockSpec((1, lc, LANES), lambda bi, i: (bi, i, 0)),
            pl.BlockSpec((1, 16, lc), lambda bi, i: (bi, 0, i)),
        ],
        out_shape=[
            jax.ShapeDtypeStruct((b, s, LANES), F32),
            jax.ShapeDtypeStruct((b, 16, s), F32),
        ],
        compiler_params=_cparams(("arbitrary", "arbitrary")),
        name="mlstm_gates",
    )(gcol, grow, bcol, jnp.broadcast_to(b_gates[:, None], (16, lc)).astype(F32))


def _mlstm_kernel(qf_ref, ktf_ref, vf_ref, gcf_ref, grf_ref, qb_ref, ktb_ref, vb_ref, gcb_ref, grb_ref,
                  hf_ref, hb_ref, c_ref, m_ref, *, lc):
    step = pl.program_id(1)
    nh, dh = M_HEADS, M_HEAD_DIM

    @pl.when(step == 0)
    def _():
        c_ref[...] = jnp.zeros(c_ref.shape, F32)
        m_ref[...] = jnp.zeros(m_ref.shape, F32)

    row_i = lax.broadcasted_iota(jnp.int32, (lc, lc), 0)
    col_j = lax.broadcasted_iota(jnp.int32, (lc, lc), 1)
    lane = lax.broadcasted_iota(jnp.int32, (lc, LANES), 1)
    ones_col = jnp.where(lane == 0, 1.0, 0.0).astype(BF16)
    dirs = ((qf_ref, ktf_ref, vf_ref, gcf_ref, grf_ref, hf_ref, col_j <= row_i),
            (qb_ref, ktb_ref, vb_ref, gcb_ref, grb_ref, hb_ref, col_j >= row_i))
    for d, (q_ref, kt_ref, v_ref, gc_ref, gr_ref, h_ref, mask) in enumerate(dirs):
        for hd in range(nh):
            sl = slice(hd * dh, (hd + 1) * dh)
            q = q_ref[0, :, sl]
            kt = kt_ref[0, sl, :]
            v = v_ref[0, :, sl]
            gi = (1 + 2 * d) * nh + hd
            b_col = gc_ref[0, :, gi:gi + 1]
            a_row = gr_ref[0, d * nh + hd:d * nh + hd + 1, :]
            b_row = gr_ref[0, (2 + d) * nh + hd:(2 + d) * nh + hd + 1, :]
            b_last = b_row[:, lc - 1:lc] if d == 0 else b_row[:, 0:1]
            sidx = d * nh + hd
            m_old = m_ref[sidx, :, 0:1]
            c_old = c_ref[sidx]
            d_log = jnp.where(mask, b_col + a_row, -jnp.inf)
            m_inter = b_col + m_old
            m_t = jnp.maximum(jnp.max(d_log, axis=1, keepdims=True), m_inter)
            s_mat = jnp.dot(q, kt, preferred_element_type=F32) * jnp.exp(d_log - m_t)
            w_inter = jnp.exp(m_inter - m_t)
            v_ext = jnp.concatenate([v, ones_col], axis=1)
            num_ext = (jnp.dot(s_mat.astype(BF16), v_ext, preferred_element_type=F32)
                       + w_inter * jnp.dot(q, c_old.astype(BF16), preferred_element_type=F32))
            den = num_ext[:, dh:dh + 1]
            h_ref[0, :, sl] = num_ext[:, :dh] / jnp.maximum(jnp.abs(den), jnp.exp(-m_t))
            g_row = b_last + a_row
            m_new = jnp.maximum(b_last + m_old, jnp.max(g_row, axis=1, keepdims=True))
            w_s = jnp.exp(g_row - m_new)
            decay = jnp.exp(b_last + m_old - m_new)
            kw = (kt.astype(F32) * w_s).astype(BF16)
            c_ref[sidx] = decay * c_old + jnp.dot(kw, v_ext, preferred_element_type=F32)
            m_ref[sidx] = jnp.broadcast_to(m_new, (1, LANES))


def _mlstm(q, kt, v, gcol, grow, seq):
    b, s, w = q.shape
    lc = MCHUNK
    nc, ncx = s // lc, seq // lc
    ncc = nc - ncx

    def fmap(c):
        return jnp.where(c < ncc, ncx + c, c - ncc)

    def bmap(c):
        return jnp.where(c < ncc, nc - 1 - c, ncx - 1 - (c - ncc))

    def specs(cmap):
        return [
            pl.BlockSpec((1, lc, w), lambda bi, c: (bi, cmap(c), 0)),
            pl.BlockSpec((1, w, lc), lambda bi, c: (bi, 0, cmap(c))),
            pl.BlockSpec((1, lc, w), lambda bi, c: (bi, cmap(c), 0)),
            pl.BlockSpec((1, lc, LANES), lambda bi, c: (bi, cmap(c), 0)),
            pl.BlockSpec((1, 16, lc), lambda bi, c: (bi, 0, cmap(c))),
        ]

    return pl.pallas_call(
        functools.partial(_mlstm_kernel, lc=lc),
        grid=(b, nc),
        in_specs=specs(fmap) + specs(bmap),
        out_specs=[
            pl.BlockSpec((1, lc, w), lambda bi, c: (bi, fmap(c), 0)),
            pl.BlockSpec((1, lc, w), lambda bi, c: (bi, bmap(c), 0)),
        ],
        out_shape=[jax.ShapeDtypeStruct((b, s, w), F32), jax.ShapeDtypeStruct((b, s, w), F32)],
        scratch_shapes=[
            pltpu.VMEM((2 * M_HEADS, M_HEAD_DIM, 2 * M_HEAD_DIM), F32),
            pltpu.VMEM((2 * M_HEADS, 1, LANES), F32),
        ],
        compiler_params=_cparams(("arbitrary", "arbitrary")),
        name="mlstm",
    )(q, kt, v, gcol, grow, q, kt, v, gcol, grow)


def _rms(x, g, eps=1e-6):
    return x * lax.rsqrt(jnp.mean(x * x, axis=-1, keepdims=True) + eps) * g


def _mla_prep_kernel(cq_ref, ckv_ref, kra_ref, krb_ref, cos_ref, sin_ref, qg_ref, kvg_ref, wq_ref, wkv_ref,
                     q_ref, k_ref, v_ref):
    nh, dn = A_HEADS, A_NOPE
    cos, sin = cos_ref[...], sin_ref[...]
    cqn = _rms(cq_ref[0], qg_ref[...]).astype(BF16)
    qa = jnp.dot(cqn, wq_ref[...], preferred_element_type=F32)
    ckvn = _rms(ckv_ref[0], kvg_ref[...]).astype(BF16)
    kva = jnp.dot(ckvn, wkv_ref[...], preferred_element_type=F32)
    kr = kra_ref[0] * cos + krb_ref[0] * sin
    lane = lax.broadcasted_iota(jnp.int32, kr.shape, 1)
    ones_col = jnp.where(lane == 0, 1.0, 0.0)
    for h in range(nh):
        qr = qa[:, (nh + h) * dn:(nh + h + 1) * dn] * cos + qa[:, (2 * nh + h) * dn:(2 * nh + h + 1) * dn] * sin
        q_ref[0, h] = (jnp.concatenate([qa[:, h * dn:(h + 1) * dn], qr], axis=1) * Q_SCALE).astype(BF16)
        k_ref[0, h] = jnp.concatenate([kva[:, 2 * h * dn:(2 * h + 1) * dn], kr], axis=1).astype(BF16)
        v_ref[0, h] = jnp.concatenate([kva[:, (2 * h + 1) * dn:(2 * h + 2) * dn], ones_col], axis=1).astype(BF16)


def _mla_prep(cq, ckv, kra, krb, cos_t, sin_t, qg, kvg, w_q, w_kv):
    b, s, _ = cq.shape
    tm = ROW_TILE
    nh = A_HEADS
    row = lambda w: pl.BlockSpec((1, tm, w), lambda bi, i: (bi, i, 0))
    const = lambda a: pl.BlockSpec(a.shape, lambda bi, i: (0, 0))
    head = lambda w: pl.BlockSpec((1, nh, tm, w), lambda bi, i: (bi, 0, i, 0))
    return pl.pallas_call(
        _mla_prep_kernel,
        grid=(b, s // tm),
        in_specs=[row(Q_LORA), row(KV_LORA), row(LANES), row(LANES),
                  pl.BlockSpec((tm, LANES), lambda bi, i: (i, 0)), pl.BlockSpec((tm, LANES), lambda bi, i: (i, 0)),
                  const(qg), const(kvg), const(w_q), const(w_kv)],
        out_specs=[head(2 * LANES), head(2 * LANES), head(2 * LANES)],
        out_shape=[jax.ShapeDtypeStruct((b, nh, s, 2 * LANES), BF16)] * 3,
        compiler_params=_cparams(("arbitrary", "arbitrary")),
        name="mla_prep",
    )(cq, ckv, kra, krb, cos_t, sin_t, qg, kvg, w_q, w_kv)


ATT_TQ = 512
ATT_TK = 768


def _attn_kernel(q_ref, k_ref, v_ref, o_ref, *, tk, nk):
    q = q_ref[0, 0]
    tq = q.shape[0]
    m = jnp.full((tq, 1), -jnp.inf, F32)
    acc = jnp.zeros((tq, 2 * A_V), F32)
    for j in range(nk):
        k = k_ref[0, 0, j * tk:(j + 1) * tk, :]
        v = v_ref[0, 0, j * tk:(j + 1) * tk, :]
        s = lax.dot_general(q, k, (((1,), (1,)), ((), ())), preferred_element_type=F32)
        m_new = jnp.maximum(m, jnp.max(s, axis=1, keepdims=True))
        p = jnp.exp2(s - m_new)
        acc = jnp.exp2(m - m_new) * acc + jnp.dot(p.astype(BF16), v, preferred_element_type=F32)
        m = m_new
    o_ref[0] = (acc[:, :A_V] / acc[:, A_V:A_V + 1]).astype(BF16)


def _attention(q, k, v, seq):
    b, nh, s, dq = q.shape
    tq, tk = ATT_TQ, ATT_TK
    ctx_len = s - seq
    cblk = seq // ctx_len
    a_x = pl.pallas_call(
        functools.partial(_attn_kernel, tk=tk, nk=s // tk),
        grid=(b, nh, seq // tq),
        in_specs=[
            pl.BlockSpec((1, 1, tq, dq), lambda bi, h, i: (bi, h, i, 0)),
            pl.BlockSpec((1, 1, s, dq), lambda bi, h, i: (bi, h, 0, 0)),
            pl.BlockSpec((1, 1, s, dq), lambda bi, h, i: (bi, h, 0, 0)),
        ],
        out_specs=pl.BlockSpec((1, tq, A_V), lambda bi, h, i: (bi, i, h)),
        out_shape=jax.ShapeDtypeStruct((b, seq, nh * A_V), BF16),
        compiler_params=_cparams(("arbitrary", "arbitrary", "arbitrary")),
        name="mla_attention",
    )(q, k, v)
    ctx_spec = pl.BlockSpec((1, 1, ctx_len, dq), lambda bi, h: (bi, h, cblk, 0))
    a_c = pl.pallas_call(
        functools.partial(_attn_kernel, tk=ctx_len, nk=1),
        grid=(b, nh),
        in_specs=[ctx_spec, ctx_spec, ctx_spec],
        out_specs=pl.BlockSpec((1, ctx_len, A_V), lambda bi, h: (bi, 0, h)),
        out_shape=jax.ShapeDtypeStruct((b, ctx_len, nh * A_V), BF16),
        compiler_params=_cparams(("arbitrary", "arbitrary")),
        name="mla_attention_ctx",
    )(q, k, v)
    return a_x, a_c


def _outproj_kernel(hf_ref, hb_ref, o_ref, ax_ref, ac_ref, x_ref, mod_ref, wout_ref, mn_ref, g_ref, b_ref, wr_ref,
                    x1_ref, h2_ref, afft_ref, *, n_xblk):
    i = pl.program_id(1)
    is_ctx = i >= n_xblk
    mod = lambda k: jnp.where(is_ctx, mod_ref[0, 1, k:k + 1, :], mod_ref[0, 0, k:k + 1, :])
    hsum = hf_ref[0] + hb_ref[0]
    dh = M_HEAD_DIM
    hn = jnp.concatenate([_ln(hsum[:, h * dh:(h + 1) * dh], 1e-6) for h in range(M_HEADS)], axis=1)
    o = o_ref[0]
    m_out = hn * mn_ref[...] * (1.0 / (1.0 + jnp.exp(-o)))
    cat = jnp.concatenate([m_out.astype(BF16), jnp.where(is_ctx, ac_ref[0], ax_ref[0])], axis=1)
    y = jnp.dot(cat, wout_ref[...], preferred_element_type=F32)
    x1 = _ln(ALPHA * x_ref[0] + mod(2) * y, 1e-5) * g_ref[...] + b_ref[...]
    x1_ref[0] = x1
    h2 = _ln(x1, 1e-6) * (1.0 + mod(4)) + mod(3)
    logits = jnp.dot(h2, wr_ref[...], precision=HIGHEST, preferred_element_type=F32)
    lane = lax.broadcasted_iota(jnp.int32, logits.shape, 1)
    logits = jnp.where(lane < N_EXPERTS, logits, -jnp.inf)
    e = jnp.exp(logits - jnp.max(logits, axis=1, keepdims=True))
    aff = e / jnp.sum(e, axis=1, keepdims=True)
    h2_ref[0] = jnp.concatenate([h2, aff], axis=1)
    afft_ref[0] = jnp.transpose(aff)[:N_EXPERTS, :]


def _outproj(hf, hb, o, a_x, a_c, xs, modtab, w_out, mn, g, bb, wr, seq, with_ctx):
    b, s, d = xs.shape
    tm = ROW_TILE
    n_xblk = seq // tm
    if not with_ctx:
        s = seq
    row = lambda w: pl.BlockSpec((1, tm, w), lambda bi, i: (bi, i, 0))
    const = lambda a: pl.BlockSpec(a.shape, lambda bi, i: (0, 0))
    return pl.pallas_call(
        functools.partial(_outproj_kernel, n_xblk=n_xblk),
        grid=(b, s // tm),
        in_specs=[row(M_WIDTH), row(M_WIDTH), row(M_WIDTH),
                  pl.BlockSpec((1, tm, A_WIDTH), lambda bi, i: (bi, jnp.minimum(i, n_xblk - 1), 0)),
                  pl.BlockSpec((1, tm, A_WIDTH), lambda bi, i: (bi, jnp.maximum(i - n_xblk, 0), 0)),
                  row(d),
                  pl.BlockSpec((1, 2, 6, d), lambda bi, i: (bi, 0, 0, 0)),
                  const(w_out), const(mn), const(g), const(bb), const(wr)],
        out_specs=[row(d), row(d + LANES), pl.BlockSpec((1, N_EXPERTS, tm), lambda bi, i: (bi, 0, i))],
        out_shape=[jax.ShapeDtypeStruct((b, s, d), F32), jax.ShapeDtypeStruct((b, s, d + LANES), F32),
                   jax.ShapeDtypeStruct((b, N_EXPERTS, s), F32)],
        compiler_params=_cparams(("arbitrary", "arbitrary")),
        name="merge_outproj_norm_router",
    )(hf, hb, o, a_x, a_c, xs, modtab, w_out, mn, g, bb, wr)


def _seg(n):
    return -(-n // 1024) * 1024


def _route_kernel(aff_ref, idx_ref, sel_ref, cnt_ref, *, cap, nrow):
    ne = N_EXPERTS
    aff = aff_ref[0]
    bits = pltpu.bitcast(aff, jnp.int32)

    def count(mask):
        return jnp.sum(jnp.sum(mask.astype(F32), axis=1, keepdims=True), axis=2, keepdims=True)

    def bit_step(i, thr):
        cand = jnp.bitwise_or(thr, lax.shift_left(jnp.int32(1), 30 - i))
        return jnp.where(count(bits >= cand) >= cap, cand, thr)

    thr = lax.fori_loop(0, 31, bit_step, jnp.zeros((ne, 1, 1), jnp.int32))
    gt = bits > thr
    eq = bits == thr
    need = cap - count(gt)

    li = lax.broadcasted_iota(jnp.int32, (LANES, LANES), 0)
    lj = lax.broadcasted_iota(jnp.int32, (LANES, LANES), 1)
    tri = (li <= lj).astype(BF16)
    nr = ne * nrow
    shift = int(math.log2(nrow))
    bi = lax.broadcasted_iota(jnp.int32, (nr, nr), 0)
    bj = lax.broadcasted_iota(jnp.int32, (nr, nr), 1)
    same = lax.shift_right_logical(bi, shift) == lax.shift_right_logical(bj, shift)
    blk = jnp.logical_and(same, bj < bi).astype(BF16)

    def prefix(mask):
        m = mask.astype(BF16).reshape(nr, LANES)
        local = jnp.dot(m, tri, preferred_element_type=F32)
        tot = jnp.broadcast_to(local[:, LANES - 1:LANES], (nr, LANES)).astype(BF16)
        offs = jnp.dot(blk, tot, preferred_element_type=F32)
        return local, offs

    eq_local, eq_offs = prefix(eq)
    eq_f = eq.astype(F32)
    eq_rank = (eq_local + eq_offs).reshape(ne, nrow, LANES) - eq_f
    sel = jnp.logical_or(gt, jnp.logical_and(eq, eq_rank < need))
    local, offs = prefix(sel)
    cnt = (local + offs).reshape(ne, nrow, LANES)
    sel_ref[0] = sel.astype(BF16)
    cnt_ref[0] = cnt.astype(jnp.int32)

    rc = jnp.max(cnt, axis=2)
    rtot = jnp.sum(sel.astype(F32), axis=2)
    rc_excl = rc - rtot
    local3 = local.reshape(ne, nrow, LANES)
    p_r = lax.broadcasted_iota(jnp.int32, (cap, nrow), 0).astype(F32)
    r_r = lax.broadcasted_iota(jnp.int32, (cap, nrow), 1).astype(F32)
    p_l = lax.broadcasted_iota(jnp.int32, (cap, LANES), 0).astype(F32)
    lane = lax.broadcasted_iota(jnp.int32, (cap, LANES), 1)
    out = jnp.zeros((cap, LANES), jnp.int32)
    for e in range(ne):
        rowsel = jnp.sum((rc[e:e + 1, :] <= p_r).astype(F32), axis=1, keepdims=True)
        onehot = (r_r == rowsel)
        in_row = jnp.dot(onehot.astype(BF16), local3[e].astype(BF16), preferred_element_type=F32)
        before = jnp.sum(jnp.where(onehot, rc_excl[e:e + 1, :], 0.0), axis=1, keepdims=True)
        lane_of = jnp.sum((in_row + before <= p_l).astype(F32), axis=1, keepdims=True)
        tok = (rowsel * LANES + lane_of).astype(jnp.int32)
        out = jnp.where(lane == e, tok, out)
    idx_ref[0] = out


def _route(aff4, cap):
    b, ne, nrow, _ = aff4.shape
    return pl.pallas_call(
        functools.partial(_route_kernel, cap=cap, nrow=nrow),
        grid=(b,),
        in_specs=[pl.BlockSpec((1, ne, nrow, LANES), lambda bi: (bi, 0, 0, 0))],
        out_specs=[pl.BlockSpec((1, cap, LANES), lambda bi: (bi, 0, 0)),
                   pl.BlockSpec((1, ne, nrow, LANES), lambda bi: (bi, 0, 0, 0)),
                   pl.BlockSpec((1, ne, nrow, LANES), lambda bi: (bi, 0, 0, 0))],
        out_shape=[jax.ShapeDtypeStruct((b, cap, LANES), jnp.int32),
                   jax.ShapeDtypeStruct((b, ne, nrow, LANES), BF16),
                   jax.ShapeDtypeStruct((b, ne, nrow, LANES), jnp.int32)],
        compiler_params=_cparams(("arbitrary",)),
        name="ec_route",
    )(aff4)


def _expert_kernel(idx_hbm, h_hbm, wg_ref, wu_ref, wd_ref, y_ref, idx_smem, xin_ref, sem_i, sem_g, *,
                   cap, s_rows, row_off, nb):
    e = pl.program_id(0)
    b = pl.program_id(1)
    d = D_MODEL
    seg = idx_smem.shape[0] // 2
    step = e * nb + b
    nsteps = N_EXPERTS * nb
    slot = lax.rem(step, 2)

    def idx_copy(st, sl):
        eb = lax.rem(st, nb) * N_EXPERTS + st // nb
        return pltpu.make_async_copy(idx_hbm.at[pl.ds(pl.multiple_of(eb * seg, seg), seg)],
                                     idx_smem.at[pl.ds(pl.multiple_of(sl * seg, seg), seg)], sem_i.at[sl])

    def issue_rows(st, sl):
        base = lax.rem(st, nb) * s_rows + row_off

        def body(p, c):
            pltpu.make_async_copy(h_hbm.at[pl.ds(base + idx_smem[sl * seg + p], 1)],
                                  xin_ref.at[sl, pl.ds(p, 1)], sem_g.at[sl]).start()
            return c

        lax.fori_loop(0, cap, body, 0)

    @pl.when(step == 0)
    def _():
        idx_copy(0, 0).start()
        idx_copy(0, 0).wait()
        issue_rows(0, 0)

        @pl.when(nsteps > 1)
        def _():
            idx_copy(1, 1).start()

    pltpu.make_async_copy(h_hbm.at[pl.ds(0, cap)], xin_ref.at[slot], sem_g.at[slot]).wait()

    @pl.when(step + 1 < nsteps)
    def _():
        idx_copy(step + 1, 1 - slot).wait()
        issue_rows(step + 1, 1 - slot)

    @pl.when(step + 2 < nsteps)
    def _():
        idx_copy(step + 2, slot).start()

    x = xin_ref[slot, :, :d].astype(BF16)
    aff = xin_ref[slot, :, d:]
    lane = lax.broadcasted_iota(jnp.int32, aff.shape, 1)
    gate = jnp.sum(jnp.where(lane == e, aff, 0.0), axis=1, keepdims=True)
    hg = jnp.dot(x, wg_ref[0], preferred_element_type=F32)
    hu = jnp.dot(x, wu_ref[0], preferred_element_type=F32)
    act = (_silu(hg) * hu).astype(BF16)
    y_ref[0] = jnp.dot(act, wd_ref[0], preferred_element_type=F32) * gate


def _experts(idx_tab, h2a, w_gate, w_up, w_down, cap, row_off):
    b, s, dw = h2a.shape
    ne, d, ff = w_gate.shape
    wspec = lambda r, c: pl.BlockSpec((1, r, c), lambda e, bi: (e, 0, 0))
    return pl.pallas_call(
        functools.partial(_expert_kernel, cap=cap, s_rows=s, row_off=row_off, nb=b),
        grid=(ne, b),
        in_specs=[pl.BlockSpec(memory_space=pl.ANY), pl.BlockSpec(memory_space=pl.ANY),
                  wspec(d, ff), wspec(d, ff), wspec(ff, d)],
        out_specs=pl.BlockSpec((1, cap, d), lambda e, bi: (bi, e, 0)),
        out_shape=jax.ShapeDtypeStruct((b, ne * cap, d), F32),
        scratch_shapes=[pltpu.SMEM((2 * _seg(cap),), jnp.int32), pltpu.VMEM((2, cap, dw), F32),
                        pltpu.SemaphoreType.DMA((2,)), pltpu.SemaphoreType.DMA((2,))],
        compiler_params=_cparams(("arbitrary", "arbitrary")),
        name="ec_experts",
    )(idx_tab, h2a.reshape(b * s, dw), w_gate, w_up, w_down)


COMBINE_TB = 128


def _combine_kernel(idx_hbm, bnd_hbm, y_hbm, sel_ref, x1_ref, mod_ref, g_ref, b_ref, *rest, cap, tb, is_ctx, nblk,
                    has_tail):
    if has_tail:
        tail_ref, o_ref, idx_smem, bnd_smem, buf_ref, sem_i, sem_g = rest
    else:
        o_ref, idx_smem, bnd_smem, buf_ref, sem_i, sem_g = rest
    b = pl.program_id(0)
    j = pl.program_id(1)
    ne = N_EXPERTS
    ybase = b * ne * cap

    def bounds(e, blk):
        return bnd_smem[e * (nblk + 1) + blk], bnd_smem[e * (nblk + 1) + blk + 1]

    def issue(blk, sl):
        for e in range(ne):
            lo, hi = bounds(e, blk)

            def body(p, c, e=e):
                tl = idx_smem[e * _seg(cap) + p] - blk * tb
                pltpu.make_async_copy(y_hbm.at[pl.ds(ybase + e * cap + p, 1)],
                                      buf_ref.at[sl, pl.ds(e * tb + tl, 1)], sem_g.at[sl]).start()
                return c

            lax.fori_loop(lo, hi, body, 0)

    def drain(blk, sl):
        n = bounds(0, blk)[1] - bounds(0, blk)[0]
        for e in range(1, ne):
            lo, hi = bounds(e, blk)
            n = n + hi - lo
        for bit in range((ne * tb).bit_length()):
            @pl.when(jnp.bitwise_and(lax.shift_right_logical(n, bit), 1) == 1)
            def _(bit=bit):
                rows = 1 << bit
                pltpu.make_async_copy(y_hbm.at[pl.ds(0, rows)], buf_ref.at[sl, pl.ds(0, rows)], sem_g.at[sl]).wait()

    @pl.when(jnp.logical_and(b == 0, j == 0))
    def _():
        buf_ref[...] = jnp.zeros(buf_ref.shape, F32)

    @pl.when(j == 0)
    def _():
        n_i, n_b = idx_smem.shape[0], bnd_smem.shape[0]
        c1 = pltpu.make_async_copy(idx_hbm.at[pl.ds(pl.multiple_of(b * n_i, n_i), n_i)], idx_smem, sem_i.at[0])
        c2 = pltpu.make_async_copy(bnd_hbm.at[pl.ds(pl.multiple_of(b * n_b, n_b), n_b)], bnd_smem, sem_i.at[1])
        c1.start()
        c2.start()
        c1.wait()
        c2.wait()
        issue(0, 0)

    @pl.when(j < nblk)
    def _():
        slot = lax.rem(j, 2)

        @pl.when(j + 1 < nblk)
        def _():
            issue(j + 1, 1 - slot)

        drain(j, slot)
        ri = lax.broadcasted_iota(jnp.int32, (tb, tb), 0)
        ci = lax.broadcasted_iota(jnp.int32, (tb, tb), 1)
        eye = (ri == ci).astype(BF16)
        mask_t = lax.dot_general(eye, sel_ref[0], (((1,), (1,)), ((), ())), preferred_element_type=F32)
        moe = jnp.zeros((tb, D_MODEL), F32)
        for e in range(ne):
            moe = moe + jnp.where(mask_t[:, e:e + 1] > 0.5, buf_ref[slot, e * tb:(e + 1) * tb, :], 0.0)
        k = 1 if is_ctx else 0
        o_ref[0] = _ln(ALPHA * x1_ref[0] + mod_ref[0, k, 5:6, :] * moe, 1e-5) * g_ref[...] + b_ref[...]

    if has_tail:
        @pl.when(j >= nblk)
        def _():
            o_ref[0] = tail_ref[0]


def _combine(idx_flat, bnd_flat, y_all, sel, x1, modtab, g, bb, cap, is_ctx, blk_off, tail=None):
    b, ne, n = sel.shape
    d = x1.shape[-1]
    tb = COMBINE_TB
    nblk = n // tb
    ntail = 0 if tail is None else tail.shape[1] // tb
    last = nblk - 1
    in_specs = [pl.BlockSpec(memory_space=pl.ANY), pl.BlockSpec(memory_space=pl.ANY), pl.BlockSpec(memory_space=pl.ANY),
                pl.BlockSpec((1, ne, tb), lambda bi, j: (bi, 0, jnp.minimum(j, last))),
                pl.BlockSpec((1, tb, d), lambda bi, j: (bi, jnp.minimum(j, last) + blk_off, 0)),
                pl.BlockSpec((1, 2, 6, d), lambda bi, j: (bi, 0, 0, 0)),
                pl.BlockSpec((1, d), lambda bi, j: (0, 0)), pl.BlockSpec((1, d), lambda bi, j: (0, 0))]
    args = [idx_flat, bnd_flat, y_all.reshape(b * ne * cap, d), sel, x1, modtab, g, bb]
    if tail is not None:
        in_specs.append(pl.BlockSpec((1, tb, d), lambda bi, j: (bi, jnp.maximum(j - nblk, 0), 0)))
        args.append(tail)
    return pl.pallas_call(
        functools.partial(_combine_kernel, cap=cap, tb=tb, is_ctx=is_ctx, nblk=nblk, has_tail=tail is not None),
        grid=(b, nblk + ntail),
        in_specs=in_specs,
        out_specs=pl.BlockSpec((1, tb, d), lambda bi, j: (bi, j, 0)),
        out_shape=jax.ShapeDtypeStruct((b, (nblk + ntail) * tb, d), F32),
        scratch_shapes=[pltpu.SMEM((ne * _seg(cap),), jnp.int32), pltpu.SMEM((_seg(ne * (nblk + 1)),), jnp.int32),
                        pltpu.VMEM((2, ne * tb, d), F32), pltpu.SemaphoreType.DMA((2,)),
                        pltpu.SemaphoreType.DMA((2,))],
        compiler_params=_cparams(("arbitrary", "arbitrary")),
        name="ec_combine_norm",
    )(*args)


def _moe_set(afft_set, h2a, x1, modtab, lw, cap, row_off, is_ctx, tail=None):
    b, ne, n = afft_set.shape
    n_pad = max(n, 16 * LANES)
    aff4 = jnp.pad(afft_set, ((0, 0), (0, 0), (0, n_pad - n)), constant_values=-1.0).reshape(b, ne, n_pad // LANES, LANES)
    idx, sel, cnt = _route(aff4, cap)
    idx_t = jnp.transpose(idx[:, :, :ne], (0, 2, 1))
    idx_flat = jnp.pad(idx_t, ((0, 0), (0, 0), (0, _seg(cap) - cap))).reshape(-1)
    nblk = n // COMBINE_TB
    ends = cnt[:, :, :nblk, LANES - 1]
    bnd = jnp.concatenate([jnp.zeros((b, ne, 1), jnp.int32), ends], axis=2).reshape(b, -1)
    bnd_flat = jnp.pad(bnd, ((0, 0), (0, _seg(bnd.shape[1]) - bnd.shape[1]))).reshape(-1)
    y_all = _experts(idx_flat, h2a, lw["w_gate"], lw["w_up"], lw["w_down"], cap, row_off)
    sel_rows = sel.reshape(b, ne, n_pad)[:, :, :n]
    return _combine(idx_flat, bnd_flat, y_all, sel_rows, x1, modtab, lw["ln2_g"], lw["ln2_b"], cap, is_ctx,
                    row_off // COMBINE_TB, tail)


def _rope_tables(seq, ctx_len):
    rows = seq // GRID_W
    half = A_ROPE // 2
    inv = ROPE_BASE ** (-jnp.arange(0, half, 2, dtype=F32) / half)
    row = jnp.repeat(jnp.arange(rows, dtype=F32), GRID_W)
    col = jnp.tile(jnp.arange(GRID_W, dtype=F32), rows)
    ang = jnp.concatenate([row[:, None] * inv, col[:, None] * inv], axis=-1)
    cos, sin = jnp.cos(ang), jnp.sin(ang)
    pad = jnp.zeros((seq, LANES - A_ROPE), F32)
    cos_x = jnp.concatenate([cos, cos, pad], axis=1)
    sin_x = jnp.concatenate([sin, sin, pad], axis=1)
    cos_c = jnp.concatenate([jnp.ones((ctx_len, A_ROPE), F32), jnp.zeros((ctx_len, LANES - A_ROPE), F32)], axis=1)
    sin_c = jnp.zeros((ctx_len, LANES), F32)
    return jnp.concatenate([cos_x, cos_c], axis=0), jnp.concatenate([sin_x, sin_c], axis=0)


def _rot(w):
    half = A_ROPE // 2
    return jnp.concatenate([-w[..., half:], w[..., :half]], axis=-1)


def _prepare(p):
    c, c_ctx = p["c"], p["c_ctx"]
    b, d = c.shape
    nl = p["w_mod"].shape[0]
    rows = -(-(b + 1) // SUBLANES) * SUBLANES
    cvec = jnp.zeros((rows, d), F32).at[:b].set(c).at[b].set(c_ctx)
    mods = _modulation(cvec, p["w_mod"], p["b_mod"]).reshape(nl, rows, 6, d)
    mx = mods[:, :b]
    mc = jnp.broadcast_to(mods[:, b:b + 1], (nl, b, 6, d))
    modtab = jnp.stack([mx, mc], axis=2)
    layers = []
    for l in range(nl):
        w_in = p["w_in"][l]
        w_kr = w_in[:, OFF_KR:IN_COLS]
        z64 = jnp.zeros((d, LANES - A_ROPE), F32)
        w_p = jnp.concatenate([
            w_in[:, OFF_Q:OFF_G], w_in[:, OFF_CQ:OFF_KR], w_kr, z64, _rot(w_kr), z64,
            w_in[:, OFF_G:OFF_CQ], jnp.zeros((d, LANES - 16), F32)], axis=1).astype(BF16)
        w_uq = p["w_uq"][l].reshape(Q_LORA, A_HEADS, A_NOPE + A_ROPE)
        zq = jnp.zeros((Q_LORA, A_HEADS, LANES - A_ROPE), F32)
        w_q = jnp.concatenate([
            w_uq[:, :, :A_NOPE].reshape(Q_LORA, -1),
            jnp.concatenate([w_uq[:, :, A_NOPE:], zq], axis=2).reshape(Q_LORA, -1),
            jnp.concatenate([_rot(w_uq[:, :, A_NOPE:]), zq], axis=2).reshape(Q_LORA, -1)], axis=1).astype(BF16)
        w_router = p["w_router"][l]
        layers.append(dict(
            w_p=w_p,
            w_g=jnp.transpose(w_in[:, OFF_G:OFF_CQ]).astype(BF16),
            b_gates=p["b_gates"][l],
            conv_w=jnp.zeros((SUBLANES, 2 * M_WIDTH), F32).at[:M_CONV].set(p["conv_w"][l]),
            conv_b=p["conv_b"][l].reshape(1, -1),
            w_q=w_q,
            w_kv=p["w_ukv"][l].astype(BF16),
            qg=p["q_norm_w"][l].reshape(1, -1),
            kvg=p["kv_norm_w"][l].reshape(1, -1),
            w_out=p["w_out"][l].astype(BF16),
            mn=p["m_norm_w"][l].reshape(1, -1),
            ln1_g=p["ln1_g"][l].reshape(1, -1),
            ln1_b=p["ln1_b"][l].reshape(1, -1),
            wr=jnp.concatenate([w_router, jnp.zeros((d, LANES - N_EXPERTS), F32)], axis=1),
            w_gate=p["w_gate"][l].astype(BF16),
            w_up=p["w_up"][l].astype(BF16),
            w_down=p["w_down"][l].astype(BF16),
            ln2_g=p["ln2_g"][l].reshape(1, -1),
            ln2_b=p["ln2_b"][l].reshape(1, -1),
        ))
    cos_t, sin_t = _rope_tables(p["x"].shape[1], p["ctx"].shape[1])
    return dict(modtab=modtab, layers=layers, cos=cos_t, sin=sin_t)


def _layer(xs, st, l, seq, last):
    b, s, d = xs.shape
    lw, modtab = st["layers"][l], st["modtab"][l]
    qk_pre, v, o, cq, ckv, kra, krb, gcol, grow = _inproj(xs, modtab, lw["w_p"], lw["w_g"], seq)
    q, kt = _conv(qk_pre, lw["conv_w"], lw["conv_b"], seq)
    gc, gr = _gate_prep(gcol, grow, lw["b_gates"])
    hf, hb = _mlstm(q, kt, v, gc, gr, seq)
    qa, ka, va = _mla_prep(cq, ckv, kra, krb, st["cos"], st["sin"], lw["qg"], lw["kvg"], lw["w_q"], lw["w_kv"])
    a_x, a_c = _attention(qa, ka, va, seq)
    x1, h2a, afft = _outproj(hf, hb, o, a_x, a_c, xs, modtab, lw["w_out"], lw["mn"], lw["ln1_g"], lw["ln1_b"],
                             lw["wr"], seq, with_ctx=not last)
    cap_x = EC_CAPACITY * seq // N_EXPERTS
    if last:
        return _moe_set(afft[:, :, :seq], h2a, x1, modtab, lw, cap_x, 0, False)
    cap_c = EC_CAPACITY * (s - seq) // N_EXPERTS
    x2_ctx = _moe_set(afft[:, :, seq:], h2a, x1, modtab, lw, cap_c, seq, True)
    return _moe_set(afft[:, :, :seq], h2a, x1, modtab, lw, cap_x, 0, False, tail=x2_ctx)


def kernel(x, c, ctx, c_ctx, w_mod, b_mod, w_in, b_gates, conv_w, conv_b, m_norm_w, q_norm_w, kv_norm_w, w_uq, w_ukv,
           w_out, ln1_g, ln1_b, w_router, w_gate, w_up, w_down, ln2_g, ln2_b):
    p = dict(x=x, c=c, ctx=ctx, c_ctx=c_ctx, w_mod=w_mod, b_mod=b_mod, w_in=w_in, b_gates=b_gates, conv_w=conv_w,
             conv_b=conv_b, m_norm_w=m_norm_w, q_norm_w=q_norm_w, kv_norm_w=kv_norm_w, w_uq=w_uq, w_ukv=w_ukv,
             w_out=w_out, ln1_g=ln1_g, ln1_b=ln1_b, w_router=w_router, w_gate=w_gate, w_up=w_up, w_down=w_down,
             ln2_g=ln2_g, ln2_b=ln2_b)
    st = _prepare(p)
    seq = x.shape[1]
    xs = jnp.concatenate([x, ctx], axis=1)
    nl = w_in.shape[0]
    for l in range(nl):
        xs = _layer(xs, st, l, seq, last=(l == nl - 1))
    return xs
```

```python
import functools
import math

import jax
import jax.numpy as jnp
from jax import lax
from jax.experimental import pallas as pl
from jax.experimental.pallas import tpu as pltpu

F32 = jnp.float32
BF16 = jnp.bfloat16
HIGHEST = lax.Precision.HIGHEST

D_MODEL = 1024
M_HEADS = 4
M_HEAD_DIM = 128
M_WIDTH = 512
M_CONV = 5
A_HEADS = 4
A_NOPE = 128
A_ROPE = 64
A_V = 128
A_WIDTH = 512
Q_LORA = 384
KV_LORA = 256
A_SCALE = (A_NOPE + A_ROPE) ** -0.5
Q_SCALE = A_SCALE * math.log2(math.e)
ROPE_BASE = 10000.0
GRID_W = 64
N_EXPERTS = 16
EXPERT_FF = 1024
EC_CAPACITY = 2
DEPTH = 2
ALPHA = (2 * DEPTH) ** 0.25
OFF_Q, OFF_K, OFF_V, OFF_O, OFF_G = 0, 512, 1024, 1536, 2048
OFF_CQ = OFF_G + 16
OFF_CKV = OFF_CQ + Q_LORA
OFF_KR = OFF_CKV + KV_LORA
IN_COLS = OFF_KR + A_ROPE

LANES = 128
SUBLANES = 8
VMEM_LIMIT = 56 * 1024 * 1024

P_QK = 0
P_V = 1024
P_O = 1536
P_CQ = 2048
P_CKV = 2432
P_KRA = 2688
P_KRB = 2816
P_G = 2944
P_COLS = 3072

ROW_TILE = 256
MCHUNK = 256


def _cparams(sem):
    return pltpu.CompilerParams(dimension_semantics=sem, vmem_limit_bytes=VMEM_LIMIT)


def _ln(x, eps):
    mu = jnp.mean(x, axis=-1, keepdims=True)
    xc = x - mu
    var = jnp.mean(xc * xc, axis=-1, keepdims=True)
    return xc * lax.rsqrt(var + eps)


def _silu(x):
    return x * (1.0 / (1.0 + jnp.exp(-x)))


def _mod_kernel(c_ref, w_ref, b_ref, o_ref):
    c = c_ref[...]
    a = _silu(c)
    o_ref[0] = jnp.dot(a, w_ref[0], precision=HIGHEST, preferred_element_type=F32) + b_ref[0]


def _modulation(cvec, w_mod, b_mod):
    nl, d, n6 = w_mod.shape
    r = cvec.shape[0]
    tn = 1536
    return pl.pallas_call(
        _mod_kernel,
        grid=(nl, n6 // tn),
        in_specs=[
            pl.BlockSpec((r, d), lambda l, j: (0, 0)),
            pl.BlockSpec((1, d, tn), lambda l, j: (l, 0, j)),
            pl.BlockSpec((1, 1, tn), lambda l, j: (l, 0, j)),
        ],
        out_specs=pl.BlockSpec((1, r, tn), lambda l, j: (l, 0, j)),
        out_shape=jax.ShapeDtypeStruct((nl, r, n6), F32),
        compiler_params=_cparams(("arbitrary", "arbitrary")),
        name="modulation",
    )(cvec, w_mod, b_mod.reshape(nl, 1, n6))


def _inproj_kernel(x_ref, mod_ref, w_ref, wg_ref, qk_ref, v_ref, o_ref, cq_ref, ckv_ref,
                   kra_ref, krb_ref, gcol_ref, grow_ref, *, n_xblk):
    i = pl.program_id(1)
    is_ctx = i >= n_xblk
    x = x_ref[0]
    shift = jnp.where(is_ctx, mod_ref[0, 1, 0:1, :], mod_ref[0, 0, 0:1, :])
    scale = jnp.where(is_ctx, mod_ref[0, 1, 1:2, :], mod_ref[0, 0, 1:2, :])
    h = (_ln(x, 1e-6) * (1.0 + scale) + shift).astype(BF16)
    p = jnp.dot(h, w_ref[...], preferred_element_type=F32)
    qk_ref[0] = p[:, P_QK:P_V]
    v_ref[0] = p[:, P_V:P_O].astype(BF16)
    o_ref[0] = p[:, P_O:P_CQ]
    cq_ref[0] = p[:, P_CQ:P_CKV]
    ckv_ref[0] = p[:, P_CKV:P_KRA]
    kra_ref[0] = p[:, P_KRA:P_KRB]
    krb_ref[0] = p[:, P_KRB:P_G]
    gcol_ref[0] = p[:, P_G:P_COLS]
    grow_ref[0] = lax.dot_general(wg_ref[...], h, (((1,), (1,)), ((), ())), preferred_element_type=F32)


def _inproj(xs, modtab, w_p, w_g, seq):
    b, s, d = xs.shape
    tm = ROW_TILE
    widths = [(1024, F32), (512, BF16), (512, F32), (Q_LORA, F32), (KV_LORA, F32), (128, F32), (128, F32),
              (128, F32)]
    out_shape = [jax.ShapeDtypeStruct((b, s, w), dt) for w, dt in widths]
    out_specs = [pl.BlockSpec((1, tm, w), lambda bi, i: (bi, i, 0)) for w, _ in widths]
    out_shape.append(jax.ShapeDtypeStruct((b, 16, s), F32))
    out_specs.append(pl.BlockSpec((1, 16, tm), lambda bi, i: (bi, 0, i)))
    return pl.pallas_call(
        functools.partial(_inproj_kernel, n_xblk=seq // tm),
        grid=(b, s // tm),
        in_specs=[
            pl.BlockSpec((1, tm, d), lambda bi, i: (bi, i, 0)),
            pl.BlockSpec((1, 2, 6, d), lambda bi, i: (bi, 0, 0, 0)),
            pl.BlockSpec((d, P_COLS), lambda bi, i: (0, 0)),
            pl.BlockSpec((16, d), lambda bi, i: (0, 0)),
        ],
        out_specs=out_specs,
        out_shape=out_shape,
        compiler_params=_cparams(("arbitrary", "arbitrary")),
        name="ln_inproj",
    )(xs, modtab, w_p, w_g)


def _conv_kernel(main_ref, prev_ref, next_ref, w_ref, b_ref, q_ref, kt_ref, ext_ref, *, tm, n_xblk, n_blk):
    i = pl.program_id(1)
    c = main_ref.shape[-1]
    first = jnp.logical_or(i == 0, i == n_xblk)
    last = jnp.logical_or(i == n_xblk - 1, i == n_blk - 1)
    zeros8 = jnp.zeros((SUBLANES, c), F32)
    ext_ref[0:SUBLANES, :] = jnp.where(first, zeros8, prev_ref[0])
    ext_ref[SUBLANES:SUBLANES + tm, :] = main_ref[0]
    ext_ref[SUBLANES + tm:, :] = jnp.where(last, zeros8, next_ref[0])
    acc = jnp.zeros((tm, c), F32) + b_ref[...]
    for j in range(M_CONV):
        off = SUBLANES - M_CONV // 2 + j
        acc = acc + ext_ref[off:off + tm, :] * w_ref[j:j + 1, :]
    y = _silu(acc)
    q_ref[0] = (y[:, :M_WIDTH] * (M_HEAD_DIM ** -0.5)).astype(BF16)
    kt_ref[0] = jnp.transpose(y[:, M_WIDTH:]).astype(BF16)


def _conv(qk_pre, conv_w, conv_b, seq):
    b, s, c = qk_pre.shape
    tm = ROW_TILE
    r8 = tm // SUBLANES
    n_blk = s // tm
    return pl.pallas_call(
        functools.partial(_conv_kernel, tm=tm, n_xblk=seq // tm, n_blk=n_blk),
        grid=(b, n_blk),
        in_specs=[
            pl.BlockSpec((1, tm, c), lambda bi, i: (bi, i, 0)),
            pl.BlockSpec((1, SUBLANES, c), lambda bi, i: (bi, jnp.maximum(i * r8 - 1, 0), 0)),
            pl.BlockSpec((1, SUBLANES, c), lambda bi, i: (bi, jnp.minimum((i + 1) * r8, s // SUBLANES - 1), 0)),
            pl.BlockSpec((SUBLANES, c), lambda bi, i: (0, 0)),
            pl.BlockSpec((1, c), lambda bi, i: (0, 0)),
        ],
        out_specs=[
            pl.BlockSpec((1, tm, M_WIDTH), lambda bi, i: (bi, i, 0)),
            pl.BlockSpec((1, M_WIDTH, tm), lambda bi, i: (bi, 0, i)),
        ],
        out_shape=[
            jax.ShapeDtypeStruct((b, s, M_WIDTH), BF16),
            jax.ShapeDtypeStruct((b, M_WIDTH, s), BF16),
        ],
        scratch_shapes=[pltpu.VMEM((tm + 2 * SUBLANES, c), F32)],
        compiler_params=_cparams(("arbitrary", "arbitrary")),
        name="conv_silu",
    )(qk_pre, qk_pre, qk_pre, conv_w, conv_b)


def _log_sigmoid(x):
    return jnp.minimum(x, 0.0) - jnp.log1p(jnp.exp(-jnp.abs(x)))


def _gate_kernel(gcol_ref, grow_ref, bcol_ref, brow_ref, ocol_ref, orow_ref, *, lc):
    nh = M_HEADS
    r = lax.broadcasted_iota(jnp.int32, (lc, lc), 0)
    c = lax.broadcasted_iota(jnp.int32, (lc, lc), 1)
    lower = (c <= r).astype(F32)
    upper = (c >= r).astype(F32)
    g = gcol_ref[0] + bcol_ref[...]
    lf = _log_sigmoid(g)
    lane = lax.broadcasted_iota(jnp.int32, g.shape, 1)
    lf_f = jnp.where(jnp.logical_and(lane >= nh, lane < 2 * nh), lf, 0.0)
    lf_b = jnp.where(jnp.logical_and(lane >= 3 * nh, lane < 4 * nh), lf, 0.0)
    cs_f = jnp.dot(lower, lf_f, precision=HIGHEST, preferred_element_type=F32)
    cs_b = jnp.dot(upper, lf_b, precision=HIGHEST, preferred_element_type=F32)
    ocol_ref[0] = cs_f + cs_b
    gr = grow_ref[0] + brow_ref[...]
    lfr = _log_sigmoid(gr)
    rs_f = jnp.dot(lfr, upper, precision=HIGHEST, preferred_element_type=F32)
    rs_b = jnp.dot(lfr, lower, precision=HIGHEST, preferred_element_type=F32)
    orow_ref[0, 0:nh, :] = gr[0:nh] - rs_f[nh:2 * nh]
    orow_ref[0, nh:2 * nh, :] = gr[2 * nh:3 * nh] - rs_b[3 * nh:4 * nh]
    orow_ref[0, 2 * nh:3 * nh, :] = rs_f[nh:2 * nh]
    orow_ref[0, 3 * nh:4 * nh, :] = rs_b[3 * nh:4 * nh]


def _gate_prep(gcol, grow, b_gates):
    b, s, _ = gcol.shape
    lc = MCHUNK
    bcol = jnp.zeros((1, LANES), F32).at[0, :16].set(b_gates)
    brow = jnp.broadcast_to(b_gates[:, None], (16, LANES)).astype(F32)
    return pl.pallas_call(
        functools.partial(_gate_kernel, lc=lc),
        grid=(b, s // lc),
        in_specs=[
            pl.BlockSpec((1, lc, LANES), lambda bi, i: (bi, i, 0)),
            pl.BlockSpec((1, 16, lc), lambda bi, i: (bi, 0, i)),
            pl.BlockSpec((1, LANES), lambda bi, i: (0, 0)),
            pl.BlockSpec((16, lc), lambda bi, i: (0, 0)),
        ],
        out_specs=[
            pl.BlockSpec((1, lc, LANES), lambda bi, i: (bi, i, 0)),
            pl.BlockSpec((1, 16, lc), lambda bi, i: (bi, 0, i)),
        ],
        out_shape=[
            jax.ShapeDtypeStruct((b, s, LANES), F32),
            jax.ShapeDtypeStruct((b, 16, s), F32),
        ],
        compiler_params=_cparams(("arbitrary", "arbitrary")),
        name="mlstm_gates",
    )(gcol, grow, bcol, jnp.broadcast_to(b_gates[:, None], (16, lc)).astype(F32))


def _mlstm_kernel(qf_ref, ktf_ref, vf_ref, gcf_ref, grf_ref, qb_ref, ktb_ref, vb_ref, gcb_ref, grb_ref,
                  hf_ref, hb_ref, c_ref, m_ref, *, lc):
    step = pl.program_id(1)
    nh, dh = M_HEADS, M_HEAD_DIM

    @pl.when(step == 0)
    def _():
        c_ref[...] = jnp.zeros(c_ref.shape, F32)
        m_ref[...] = jnp.zeros(m_ref.shape, F32)

    row_i = lax.broadcasted_iota(jnp.int32, (lc, lc), 0)
    col_j = lax.broadcasted_iota(jnp.int32, (lc, lc), 1)
    lane = lax.broadcasted_iota(jnp.int32, (lc, LANES), 1)
    ones_col = jnp.where(lane == 0, 1.0, 0.0).astype(BF16)
    dirs = ((qf_ref, ktf_ref, vf_ref, gcf_ref, grf_ref, hf_ref, col_j <= row_i),
            (qb_ref, ktb_ref, vb_ref, gcb_ref, grb_ref, hb_ref, col_j >= row_i))
    for d, (q_ref, kt_ref, v_ref, gc_ref, gr_ref, h_ref, mask) in enumerate(dirs):
        for hd in range(nh):
            sl = slice(hd * dh, (hd + 1) * dh)
            q = q_ref[0, :, sl]
            kt = kt_ref[0, sl, :]
            v = v_ref[0, :, sl]
            gi = (1 + 2 * d) * nh + hd
            b_col = gc_ref[0, :, gi:gi + 1]
            a_row = gr_ref[0, d * nh + hd:d * nh + hd + 1, :]
            b_row = gr_ref[0, (2 + d) * nh + hd:(2 + d) * nh + hd + 1, :]
            b_last = b_row[:, lc - 1:lc] if d == 0 else b_row[:, 0:1]
            sidx = d * nh + hd
            m_old = m_ref[sidx, :, 0:1]
            c_old = c_ref[sidx]
            d_log = jnp.where(mask, b_col + a_row, -jnp.inf)
            m_inter = b_col + m_old
            m_t = jnp.maximum(jnp.max(d_log, axis=1, keepdims=True), m_inter)
            s_mat = jnp.dot(q, kt, preferred_element_type=F32) * jnp.exp(d_log - m_t)
            w_inter = jnp.exp(m_inter - m_t)
            v_ext = jnp.concatenate([v, ones_col], axis=1)
            num_ext = (jnp.dot(s_mat.astype(BF16), v_ext, preferred_element_type=F32)
                       + w_inter * jnp.dot(q, c_old.astype(BF16), preferred_element_type=F32))
            den = num_ext[:, dh:dh + 1]
            h_ref[0, :, sl] = num_ext[:, :dh] / jnp.maximum(jnp.abs(den), jnp.exp(-m_t))
            g_row = b_last + a_row
            m_new = jnp.maximum(b_last + m_old, jnp.max(g_row, axis=1, keepdims=True))
            w_s = jnp.exp(g_row - m_new)
            decay = jnp.exp(b_last + m_old - m_new)
            kw = (kt.astype(F32) * w_s).astype(BF16)
            c_ref[sidx] = decay * c_old + jnp.dot(kw, v_ext, preferred_element_type=F32)
            m_ref[sidx] = jnp.broadcast_to(m_new, (1, LANES))


def _mlstm(q, kt, v, gcol, grow, seq):
    b, s, w = q.shape
    lc = MCHUNK
    nc, ncx = s // lc, seq // lc
    ncc = nc - ncx

    def fmap(c):
        return jnp.where(c < ncc, ncx + c, c - ncc)

    def bmap(c):
        return jnp.where(c < ncc, nc - 1 - c, ncx - 1 - (c - ncc))

    def specs(cmap):
        return [
            pl.BlockSpec((1, lc, w), lambda bi, c: (bi, cmap(c), 0)),
            pl.BlockSpec((1, w, lc), lambda bi, c: (bi, 0, cmap(c))),
            pl.BlockSpec((1, lc, w), lambda bi, c: (bi, cmap(c), 0)),
            pl.BlockSpec((1, lc, LANES), lambda bi, c: (bi, cmap(c), 0)),
            pl.BlockSpec((1, 16, lc), lambda bi, c: (bi, 0, cmap(c))),
        ]

    return pl.pallas_call(
        functools.partial(_mlstm_kernel, lc=lc),
        grid=(b, nc),
        in_specs=specs(fmap) + specs(bmap),
        out_specs=[
            pl.BlockSpec((1, lc, w), lambda bi, c: (bi, fmap(c), 0)),
            pl.BlockSpec((1, lc, w), lambda bi, c: (bi, bmap(c), 0)),
        ],
        out_shape=[jax.ShapeDtypeStruct((b, s, w), F32), jax.ShapeDtypeStruct((b, s, w), F32)],
        scratch_shapes=[
            pltpu.VMEM((2 * M_HEADS, M_HEAD_DIM, 2 * M_HEAD_DIM), F32),
            pltpu.VMEM((2 * M_HEADS, 1, LANES), F32),
        ],
        compiler_params=_cparams(("arbitrary", "arbitrary")),
        name="mlstm",
    )(q, kt, v, gcol, grow, q, kt, v, gcol, grow)


def _rms(x, g, eps=1e-6):
    return x * lax.rsqrt(jnp.mean(x * x, axis=-1, keepdims=True) + eps) * g


def _mla_prep_kernel(cq_ref, ckv_ref, kra_ref, krb_ref, cos_ref, sin_ref, qg_ref, kvg_ref, wq_ref, wkv_ref,
                     q_ref, k_ref, v_ref):
    nh, dn = A_HEADS, A_NOPE
    cos, sin = cos_ref[...], sin_ref[...]
    cqn = _rms(cq_ref[0], qg_ref[...]).astype(BF16)
    qa = jnp.dot(cqn, wq_ref[...], preferred_element_type=F32)
    ckvn = _rms(ckv_ref[0], kvg_ref[...]).astype(BF16)
    kva = jnp.dot(ckvn, wkv_ref[...], preferred_element_type=F32)
    kr = kra_ref[0] * cos + krb_ref[0] * sin
    lane = lax.broadcasted_iota(jnp.int32, kr.shape, 1)
    ones_col = jnp.where(lane == 0, 1.0, 0.0)
    for h in range(nh):
        qr = qa[:, (nh + h) * dn:(nh + h + 1) * dn] * cos + qa[:, (2 * nh + h) * dn:(2 * nh + h + 1) * dn] * sin
        q_ref[0, h] = (jnp.concatenate([qa[:, h * dn:(h + 1) * dn], qr], axis=1) * Q_SCALE).astype(BF16)
        k_ref[0, h] = jnp.concatenate([kva[:, 2 * h * dn:(2 * h + 1) * dn], kr], axis=1).astype(BF16)
        v_ref[0, h] = jnp.concatenate([kva[:, (2 * h + 1) * dn:(2 * h + 2) * dn], ones_col], axis=1).astype(BF16)


def _mla_prep(cq, ckv, kra, krb, cos_t, sin_t, qg, kvg, w_q, w_kv):
    b, s, _ = cq.shape
    tm = ROW_TILE
    nh = A_HEADS
    row = lambda w: pl.BlockSpec((1, tm, w), lambda bi, i: (bi, i, 0))
    const = lambda a: pl.BlockSpec(a.shape, lambda bi, i: (0, 0))
    head = lambda w: pl.BlockSpec((1, nh, tm, w), lambda bi, i: (bi, 0, i, 0))
    return pl.pallas_call(
        _mla_prep_kernel,
        grid=(b, s // tm),
        in_specs=[row(Q_LORA), row(KV_LORA), row(LANES), row(LANES),
                  pl.BlockSpec((tm, LANES), lambda bi, i: (i, 0)), pl.BlockSpec((tm, LANES), lambda bi, i: (i, 0)),
                  const(qg), const(kvg), const(w_q), const(w_kv)],
        out_specs=[head(2 * LANES), head(2 * LANES), head(2 * LANES)],
        out_shape=[jax.ShapeDtypeStruct((b, nh, s, 2 * LANES), BF16)] * 3,
        compiler_params=_cparams(("arbitrary", "arbitrary")),
        name="mla_prep",
    )(cq, ckv, kra, krb, cos_t, sin_t, qg, kvg, w_q, w_kv)


ATT_TQ = 512
ATT_TK = 768


def _attn_kernel(q_ref, k_ref, v_ref, o_ref, *, tk, nk):
    q = q_ref[0, 0]
    tq = q.shape[0]
    m = jnp.full((tq, 1), -jnp.inf, F32)
    acc = jnp.zeros((tq, 2 * A_V), F32)
    for j in range(nk):
        k = k_ref[0, 0, j * tk:(j + 1) * tk, :]
        v = v_ref[0, 0, j * tk:(j + 1) * tk, :]
        s = lax.dot_general(q, k, (((1,), (1,)), ((), ())), preferred_element_type=F32)
        m_new = jnp.maximum(m, jnp.max(s, axis=1, keepdims=True))
        p = jnp.exp2(s - m_new)
        acc = jnp.exp2(m - m_new) * acc + jnp.dot(p.astype(BF16), v, preferred_element_type=F32)
        m = m_new
    o_ref[0] = (acc[:, :A_V] / acc[:, A_V:A_V + 1]).astype(BF16)


def _attention(q, k, v, seq):
    b, nh, s, dq = q.shape
    tq, tk = ATT_TQ, ATT_TK
    ctx_len = s - seq
    cblk = seq // ctx_len
    a_x = pl.pallas_call(
        functools.partial(_attn_kernel, tk=tk, nk=s // tk),
        grid=(b, nh, seq // tq),
        in_specs=[
            pl.BlockSpec((1, 1, tq, dq), lambda bi, h, i: (bi, h, i, 0)),
            pl.BlockSpec((1, 1, s, dq), lambda bi, h, i: (bi, h, 0, 0)),
            pl.BlockSpec((1, 1, s, dq), lambda bi, h, i: (bi, h, 0, 0)),
        ],
        out_specs=pl.BlockSpec((1, tq, A_V), lambda bi, h, i: (bi, i, h)),
        out_shape=jax.ShapeDtypeStruct((b, seq, nh * A_V), BF16),
        compiler_params=_cparams(("arbitrary", "arbitrary", "arbitrary")),
        name="mla_attention",
    )(q, k, v)
    ctx_spec = pl.BlockSpec((1, 1, ctx_len, dq), lambda bi, h: (bi, h, cblk, 0))
    a_c = pl.pallas_call(
        functools.partial(_attn_kernel, tk=ctx_len, nk=1),
        grid=(b, nh),
        in_specs=[ctx_spec, ctx_spec, ctx_spec],
        out_specs=pl.BlockSpec((1, ctx_len, A_V), lambda bi, h: (bi, 0, h)),
        out_shape=jax.ShapeDtypeStruct((b, ctx_len, nh * A_V), BF16),
        compiler_params=_cparams(("arbitrary", "arbitrary")),
        name="mla_attention_ctx",
    )(q, k, v)
    return a_x, a_c


def _outproj_kernel(hf_ref, hb_ref, o_ref, ax_ref, ac_ref, x_ref, mod_ref, wout_ref, mn_ref, g_ref, b_ref, wr_ref,
                    x1_ref, h2_ref, afft_ref, *, n_xblk):
    i = pl.program_id(1)
    is_ctx = i >= n_xblk
    mod = lambda k: jnp.where(is_ctx, mod_ref[0, 1, k:k + 1, :], mod_ref[0, 0, k:k + 1, :])
    hsum = hf_ref[0] + hb_ref[0]
    dh = M_HEAD_DIM
    hn = jnp.concatenate([_ln(hsum[:, h * dh:(h + 1) * dh], 1e-6) for h in range(M_HEADS)], axis=1)
    o = o_ref[0]
    m_out = hn * mn_ref[...] * (1.0 / (1.0 + jnp.exp(-o)))
    cat = jnp.concatenate([m_out.astype(BF16), jnp.where(is_ctx, ac_ref[0], ax_ref[0])], axis=1)
    y = jnp.dot(cat, wout_ref[...], preferred_element_type=F32)
    x1 = _ln(ALPHA * x_ref[0] + mod(2) * y, 1e-5) * g_ref[...] + b_ref[...]
    x1_ref[0] = x1
    h2 = _ln(x1, 1e-6) * (1.0 + mod(4)) + mod(3)
    logits = jnp.dot(h2, wr_ref[...], precision=HIGHEST, preferred_element_type=F32)
    lane = lax.broadcasted_iota(jnp.int32, logits.shape, 1)
    logits = jnp.where(lane < N_EXPERTS, logits, -jnp.inf)
    e = jnp.exp(logits - jnp.max(logits, axis=1, keepdims=True))
    aff = e / jnp.sum(e, axis=1, keepdims=True)
    h2_ref[0] = jnp.concatenate([h2, aff], axis=1)
    afft_ref[0] = jnp.transpose(aff)[:N_EXPERTS, :]


def _outproj(hf, hb, o, a_x, a_c, xs, modtab, w_out, mn, g, bb, wr, seq, with_ctx):
    b, s, d = xs.shape
    tm = ROW_TILE
    n_xblk = seq // tm
    if not with_ctx:
        s = seq
    row = lambda w: pl.BlockSpec((1, tm, w), lambda bi, i: (bi, i, 0))
    const = lambda a: pl.BlockSpec(a.shape, lambda bi, i: (0, 0))
    return pl.pallas_call(
        functools.partial(_outproj_kernel, n_xblk=n_xblk),
        grid=(b, s // tm),
        in_specs=[row(M_WIDTH), row(M_WIDTH), row(M_WIDTH),
                  pl.BlockSpec((1, tm, A_WIDTH), lambda bi, i: (bi, jnp.minimum(i, n_xblk - 1), 0)),
                  pl.BlockSpec((1, tm, A_WIDTH), lambda bi, i: (bi, jnp.maximum(i - n_xblk, 0), 0)),
                  row(d),
                  pl.BlockSpec((1, 2, 6, d), lambda bi, i: (bi, 0, 0, 0)),
                  const(w_out), const(mn), const(g), const(bb), const(wr)],
        out_specs=[row(d), row(d + LANES), pl.BlockSpec((1, N_EXPERTS, tm), lambda bi, i: (bi, 0, i))],
        out_shape=[jax.ShapeDtypeStruct((b, s, d), F32), jax.ShapeDtypeStruct((b, s, d + LANES), F32),
                   jax.ShapeDtypeStruct((b, N_EXPERTS, s), F32)],
        compiler_params=_cparams(("arbitrary", "arbitrary")),
        name="merge_outproj_norm_router",
    )(hf, hb, o, a_x, a_c, xs, modtab, w_out, mn, g, bb, wr)


def _seg(n):
    return -(-n // 1024) * 1024


def _route_kernel(aff_ref, idx_ref, sel_ref, cnt_ref, *, cap, nrow):
    ne = N_EXPERTS
    aff = aff_ref[0]
    bits = pltpu.bitcast(aff, jnp.int32)

    def count(mask):
        return jnp.sum(jnp.sum(mask.astype(F32), axis=1, keepdims=True), axis=2, keepdims=True)

    def bit_step(i, thr):
        cand = jnp.bitwise_or(thr, lax.shift_left(jnp.int32(1), 30 - i))
        return jnp.where(count(bits >= cand) >= cap, cand, thr)

    thr = lax.fori_loop(0, 31, bit_step, jnp.zeros((ne, 1, 1), jnp.int32))
    gt = bits > thr
    eq = bits == thr
    need = cap - count(gt)

    li = lax.broadcasted_iota(jnp.int32, (LANES, LANES), 0)
    lj = lax.broadcasted_iota(jnp.int32, (LANES, LANES), 1)
    tri = (li <= lj).astype(BF16)
    nr = ne * nrow
    shift = int(math.log2(nrow))
    bi = lax.broadcasted_iota(jnp.int32, (nr, nr), 0)
    bj = lax.broadcasted_iota(jnp.int32, (nr, nr), 1)
    same = lax.shift_right_logical(bi, shift) == lax.shift_right_logical(bj, shift)
    blk = jnp.logical_and(same, bj < bi).astype(BF16)

    def prefix(mask):
        m = mask.astype(BF16).reshape(nr, LANES)
        local = jnp.dot(m, tri, preferred_element_type=F32)
        tot = jnp.broadcast_to(local[:, LANES - 1:LANES], (nr, LANES)).astype(BF16)
        offs = jnp.dot(blk, tot, preferred_element_type=F32)
        return local, offs

    eq_local, eq_offs = prefix(eq)
    eq_f = eq.astype(F32)
    eq_rank = (eq_local + eq_offs).reshape(ne, nrow, LANES) - eq_f
    sel = jnp.logical_or(gt, jnp.logical_and(eq, eq_rank < need))
    local, offs = prefix(sel)
    cnt = (local + offs).reshape(ne, nrow, LANES)
    sel_ref[0] = sel.astype(BF16)
    cnt_ref[0] = cnt.astype(jnp.int32)

    rc = jnp.max(cnt, axis=2)
    rtot = jnp.sum(sel.astype(F32), axis=2)
    rc_excl = rc - rtot
    local3 = local.reshape(ne, nrow, LANES)
    p_r = lax.broadcasted_iota(jnp.int32, (cap, nrow), 0).astype(F32)
    r_r = lax.broadcasted_iota(jnp.int32, (cap, nrow), 1).astype(F32)
    p_l = lax.broadcasted_iota(jnp.int32, (cap, LANES), 0).astype(F32)
    lane = lax.broadcasted_iota(jnp.int32, (cap, LANES), 1)
    out = jnp.zeros((cap, LANES), jnp.int32)
    for e in range(ne):
        rowsel = jnp.sum((rc[e:e + 1, :] <= p_r).astype(F32), axis=1, keepdims=True)
        onehot = (r_r == rowsel)
        in_row = jnp.dot(onehot.astype(BF16), local3[e].astype(BF16), preferred_element_type=F32)
        before = jnp.sum(jnp.where(onehot, rc_excl[e:e + 1, :], 0.0), axis=1, keepdims=True)
        lane_of = jnp.sum((in_row + before <= p_l).astype(F32), axis=1, keepdims=True)
        tok = (rowsel * LANES + lane_of).astype(jnp.int32)
        out = jnp.where(lane == e, tok, out)
    idx_ref[0] = out


def _route(aff4, cap):
    b, ne, nrow, _ = aff4.shape
    return pl.pallas_call(
        functools.partial(_route_kernel, cap=cap, nrow=nrow),
        grid=(b,),
        in_specs=[pl.BlockSpec((1, ne, nrow, LANES), lambda bi: (bi, 0, 0, 0))],
        out_specs=[pl.BlockSpec((1, cap, LANES), lambda bi: (bi, 0, 0)),
                   pl.BlockSpec((1, ne, nrow, LANES), lambda bi: (bi, 0, 0, 0)),
                   pl.BlockSpec((1, ne, nrow, LANES), lambda bi: (bi, 0, 0, 0))],
        out_shape=[jax.ShapeDtypeStruct((b, cap, LANES), jnp.int32),
                   jax.ShapeDtypeStruct((b, ne, nrow, LANES), BF16),
                   jax.ShapeDtypeStruct((b, ne, nrow, LANES), jnp.int32)],
        compiler_params=_cparams(("arbitrary",)),
        name="ec_route",
    )(aff4)


def _expert_kernel(idx_hbm, h_hbm, wg_ref, wu_ref, wd_ref, y_ref, idx0, idx1, xin0, xin1, sem_i, sem_g, *,
                   cap, s_rows, row_off, nb):
    e = pl.program_id(0)
    b = pl.program_id(1)
    d = D_MODEL
    seg = idx0.shape[0]
    step = e * nb + b
    nsteps = N_EXPERTS * nb
    idx_bufs, x_bufs = (idx0, idx1), (xin0, xin1)

    def idx_copy(st, sl):
        eb = lax.rem(st, nb) * N_EXPERTS + st // nb
        return pltpu.make_async_copy(idx_hbm.at[pl.ds(pl.multiple_of(eb * seg, seg), seg)], idx_bufs[sl],
                                     sem_i.at[sl])

    def row_copy(st, sl, p):
        base = lax.rem(st, nb) * s_rows + row_off
        return pltpu.make_async_copy(h_hbm.at[pl.ds(base + idx_bufs[sl][p], 1)], x_bufs[sl].at[pl.ds(p, 1)],
                                     sem_g.at[sl])

    def rows_done(sl):
        return pltpu.make_async_copy(h_hbm.at[pl.ds(0, cap)], x_bufs[sl], sem_g.at[sl])

    @pl.when(step == 0)
    def _():
        idx_copy(0, 0).start()
        idx_copy(0, 0).wait()

        def body(p, c):
            row_copy(0, 0, p).start()
            return c

        lax.fori_loop(0, cap, body, 0)
        idx_copy(1, 1).start()

    def run(sl):
        nxt = lax.rem(step + 1, nsteps)
        rows_done(sl).wait()
        idx_copy(nxt, 1 - sl).wait()
        for p in range(cap):
            row_copy(nxt, 1 - sl, p).start()
        idx_copy(lax.rem(step + 2, nsteps), sl).start()
        x = x_bufs[sl][:, :d].astype(BF16)
        aff = x_bufs[sl][:, d:]
        lane = lax.broadcasted_iota(jnp.int32, aff.shape, 1)
        gate = jnp.sum(jnp.where(lane == e, aff, 0.0), axis=1, keepdims=True)
        hg = jnp.dot(x, wg_ref[0], preferred_element_type=F32)
        hu = jnp.dot(x, wu_ref[0], preferred_element_type=F32)
        act = (_silu(hg) * hu).astype(BF16)
        y_ref[0] = jnp.dot(act, wd_ref[0], preferred_element_type=F32) * gate

        @pl.when(step == nsteps - 1)
        def _():
            rows_done(1 - sl).wait()
            idx_copy(0, sl).wait()

    for sl in range(2):
        pl.when(lax.rem(step, 2) == sl)(functools.partial(run, sl))


def _experts(idx_tab, h2a, w_gate, w_up, w_down, cap, row_off):
    b, s, dw = h2a.shape
    ne, d, ff = w_gate.shape
    wspec = lambda r, c: pl.BlockSpec((1, r, c), lambda e, bi: (e, 0, 0))
    return pl.pallas_call(
        functools.partial(_expert_kernel, cap=cap, s_rows=s, row_off=row_off, nb=b),
        grid=(ne, b),
        in_specs=[pl.BlockSpec(memory_space=pl.ANY), pl.BlockSpec(memory_space=pl.ANY),
                  wspec(d, ff), wspec(d, ff), wspec(ff, d)],
        out_specs=pl.BlockSpec((1, cap, d), lambda e, bi: (bi, e, 0)),
        out_shape=jax.ShapeDtypeStruct((b, ne * cap, d), F32),
        scratch_shapes=[pltpu.SMEM((_seg(cap),), jnp.int32), pltpu.SMEM((_seg(cap),), jnp.int32),
                        pltpu.VMEM((cap, dw), F32), pltpu.VMEM((cap, dw), F32),
                        pltpu.SemaphoreType.DMA((2,)), pltpu.SemaphoreType.DMA((2,))],
        compiler_params=_cparams(("arbitrary", "arbitrary")),
        name="ec_experts",
    )(idx_tab, h2a.reshape(b * s, dw), w_gate, w_up, w_down)


COMBINE_TB = 128


def _combine_kernel(idx_hbm, bnd_hbm, y_hbm, sel_ref, x1_ref, mod_ref, g_ref, b_ref, *rest, cap, tb, is_ctx, nblk,
                    has_tail):
    if has_tail:
        tail_ref, o_ref, idx_smem, bnd_smem, buf_ref, sem_i, sem_g = rest
    else:
        o_ref, idx_smem, bnd_smem, buf_ref, sem_i, sem_g = rest
    b = pl.program_id(0)
    j = pl.program_id(1)
    ne = N_EXPERTS
    ybase = b * ne * cap

    def bounds(e, blk):
        return bnd_smem[e * (nblk + 1) + blk], bnd_smem[e * (nblk + 1) + blk + 1]

    def issue(blk, sl):
        for e in range(ne):
            lo, hi = bounds(e, blk)

            def body(p, c, e=e):
                tl = idx_smem[e * _seg(cap) + p] - blk * tb
                pltpu.make_async_copy(y_hbm.at[pl.ds(ybase + e * cap + p, 1)],
                                      buf_ref.at[sl, pl.ds(e * tb + tl, 1)], sem_g.at[sl]).start()
                return c

            lax.fori_loop(lo, hi, body, 0)

    def drain(blk, sl):
        n = bounds(0, blk)[1] - bounds(0, blk)[0]
        for e in range(1, ne):
            lo, hi = bounds(e, blk)
            n = n + hi - lo
        for bit in range((ne * tb).bit_length()):
            @pl.when(jnp.bitwise_and(lax.shift_right_logical(n, bit), 1) == 1)
            def _(bit=bit):
                rows = 1 << bit
                pltpu.make_async_copy(y_hbm.at[pl.ds(0, rows)], buf_ref.at[sl, pl.ds(0, rows)], sem_g.at[sl]).wait()

    @pl.when(jnp.logical_and(b == 0, j == 0))
    def _():
        buf_ref[...] = jnp.zeros(buf_ref.shape, F32)

    @pl.when(j == 0)
    def _():
        n_i, n_b = idx_smem.shape[0], bnd_smem.shape[0]
        c1 = pltpu.make_async_copy(idx_hbm.at[pl.ds(pl.multiple_of(b * n_i, n_i), n_i)], idx_smem, sem_i.at[0])
        c2 = pltpu.make_async_copy(bnd_hbm.at[pl.ds(pl.multiple_of(b * n_b, n_b), n_b)], bnd_smem, sem_i.at[1])
        c1.start()
        c2.start()
        c1.wait()
        c2.wait()
        issue(0, 0)

    @pl.when(j < nblk)
    def _():
        slot = lax.rem(j, 2)

        @pl.when(j + 1 < nblk)
        def _():
            issue(j + 1, 1 - slot)

        drain(j, slot)
        ri = lax.broadcasted_iota(jnp.int32, (tb, tb), 0)
        ci = lax.broadcasted_iota(jnp.int32, (tb, tb), 1)
        eye = (ri == ci).astype(BF16)
        mask_t = lax.dot_general(eye, sel_ref[0], (((1,), (1,)), ((), ())), preferred_element_type=F32)
        moe = jnp.zeros((tb, D_MODEL), F32)
        for e in range(ne):
            moe = moe + jnp.where(mask_t[:, e:e + 1] > 0.5, buf_ref[slot, e * tb:(e + 1) * tb, :], 0.0)
        k = 1 if is_ctx else 0
        o_ref[0] = _ln(ALPHA * x1_ref[0] + mod_ref[0, k, 5:6, :] * moe, 1e-5) * g_ref[...] + b_ref[...]

    if has_tail:
        @pl.when(j >= nblk)
        def _():
            o_ref[0] = tail_ref[0]


def _combine(idx_flat, bnd_flat, y_all, sel, x1, modtab, g, bb, cap, is_ctx, blk_off, tail=None):
    b, ne, n = sel.shape
    d = x1.shape[-1]
    tb = COMBINE_TB
    nblk = n // tb
    ntail = 0 if tail is None else tail.shape[1] // tb
    last = nblk - 1
    in_specs = [pl.BlockSpec(memory_space=pl.ANY), pl.BlockSpec(memory_space=pl.ANY), pl.BlockSpec(memory_space=pl.ANY),
                pl.BlockSpec((1, ne, tb), lambda bi, j: (bi, 0, jnp.minimum(j, last))),
                pl.BlockSpec((1, tb, d), lambda bi, j: (bi, jnp.minimum(j, last) + blk_off, 0)),
                pl.BlockSpec((1, 2, 6, d), lambda bi, j: (bi, 0, 0, 0)),
                pl.BlockSpec((1, d), lambda bi, j: (0, 0)), pl.BlockSpec((1, d), lambda bi, j: (0, 0))]
    args = [idx_flat, bnd_flat, y_all.reshape(b * ne * cap, d), sel, x1, modtab, g, bb]
    if tail is not None:
        in_specs.append(pl.BlockSpec((1, tb, d), lambda bi, j: (bi, jnp.maximum(j - nblk, 0), 0)))
        args.append(tail)
    return pl.pallas_call(
        functools.partial(_combine_kernel, cap=cap, tb=tb, is_ctx=is_ctx, nblk=nblk, has_tail=tail is not None),
        grid=(b, nblk + ntail),
        in_specs=in_specs,
        out_specs=pl.BlockSpec((1, tb, d), lambda bi, j: (bi, j, 0)),
        out_shape=jax.ShapeDtypeStruct((b, (nblk + ntail) * tb, d), F32),
        scratch_shapes=[pltpu.SMEM((ne * _seg(cap),), jnp.int32), pltpu.SMEM((_seg(ne * (nblk + 1)),), jnp.int32),
                        pltpu.VMEM((2, ne * tb, d), F32), pltpu.SemaphoreType.DMA((2,)),
                        pltpu.SemaphoreType.DMA((2,))],
        compiler_params=_cparams(("arbitrary", "arbitrary")),
        name="ec_combine_norm",
    )(*args)


def _moe_set(afft_set, h2a, x1, modtab, lw, cap, row_off, is_ctx, tail=None):
    b, ne, n = afft_set.shape
    n_pad = max(n, 16 * LANES)
    aff4 = jnp.pad(afft_set, ((0, 0), (0, 0), (0, n_pad - n)), constant_values=-1.0).reshape(b, ne, n_pad // LANES, LANES)
    idx, sel, cnt = _route(aff4, cap)
    idx_t = jnp.transpose(idx[:, :, :ne], (0, 2, 1))
    idx_flat = jnp.pad(idx_t, ((0, 0), (0, 0), (0, _seg(cap) - cap))).reshape(-1)
    nblk = n // COMBINE_TB
    ends = cnt[:, :, :nblk, LANES - 1]
    bnd = jnp.concatenate([jnp.zeros((b, ne, 1), jnp.int32), ends], axis=2).reshape(b, -1)
    bnd_flat = jnp.pad(bnd, ((0, 0), (0, _seg(bnd.shape[1]) - bnd.shape[1]))).reshape(-1)
    y_all = _experts(idx_flat, h2a, lw["w_gate"], lw["w_up"], lw["w_down"], cap, row_off)
    sel_rows = sel.reshape(b, ne, n_pad)[:, :, :n]
    return _combine(idx_flat, bnd_flat, y_all, sel_rows, x1, modtab, lw["ln2_g"], lw["ln2_b"], cap, is_ctx,
                    row_off // COMBINE_TB, tail)


def _rope_tables(seq, ctx_len):
    rows = seq // GRID_W
    half = A_ROPE // 2
    inv = ROPE_BASE ** (-jnp.arange(0, half, 2, dtype=F32) / half)
    row = jnp.repeat(jnp.arange(rows, dtype=F32), GRID_W)
    col = jnp.tile(jnp.arange(GRID_W, dtype=F32), rows)
    ang = jnp.concatenate([row[:, None] * inv, col[:, None] * inv], axis=-1)
    cos, sin = jnp.cos(ang), jnp.sin(ang)
    pad = jnp.zeros((seq, LANES - A_ROPE), F32)
    cos_x = jnp.concatenate([cos, cos, pad], axis=1)
    sin_x = jnp.concatenate([sin, sin, pad], axis=1)
    cos_c = jnp.concatenate([jnp.ones((ctx_len, A_ROPE), F32), jnp.zeros((ctx_len, LANES - A_ROPE), F32)], axis=1)
    sin_c = jnp.zeros((ctx_len, LANES), F32)
    return jnp.concatenate([cos_x, cos_c], axis=0), jnp.concatenate([sin_x, sin_c], axis=0)


def _rot(w):
    half = A_ROPE // 2
    return jnp.concatenate([-w[..., half:], w[..., :half]], axis=-1)


def _prepare(p):
    c, c_ctx = p["c"], p["c_ctx"]
    b, d = c.shape
    nl = p["w_mod"].shape[0]
    rows = -(-(b + 1) // SUBLANES) * SUBLANES
    cvec = jnp.zeros((rows, d), F32).at[:b].set(c).at[b].set(c_ctx)
    mods = _modulation(cvec, p["w_mod"], p["b_mod"]).reshape(nl, rows, 6, d)
    mx = mods[:, :b]
    mc = jnp.broadcast_to(mods[:, b:b + 1], (nl, b, 6, d))
    modtab = jnp.stack([mx, mc], axis=2)
    layers = []
    for l in range(nl):
        w_in = p["w_in"][l]
        w_kr = w_in[:, OFF_KR:IN_COLS]
        z64 = jnp.zeros((d, LANES - A_ROPE), F32)
        w_p = jnp.concatenate([
            w_in[:, OFF_Q:OFF_G], w_in[:, OFF_CQ:OFF_KR], w_kr, z64, _rot(w_kr), z64,
            w_in[:, OFF_G:OFF_CQ], jnp.zeros((d, LANES - 16), F32)], axis=1).astype(BF16)
        w_uq = p["w_uq"][l].reshape(Q_LORA, A_HEADS, A_NOPE + A_ROPE)
        zq = jnp.zeros((Q_LORA, A_HEADS, LANES - A_ROPE), F32)
        w_q = jnp.concatenate([
            w_uq[:, :, :A_NOPE].reshape(Q_LORA, -1),
            jnp.concatenate([w_uq[:, :, A_NOPE:], zq], axis=2).reshape(Q_LORA, -1),
            jnp.concatenate([_rot(w_uq[:, :, A_NOPE:]), zq], axis=2).reshape(Q_LORA, -1)], axis=1).astype(BF16)
        w_router = p["w_router"][l]
        layers.append(dict(
            w_p=w_p,
            w_g=jnp.transpose(w_in[:, OFF_G:OFF_CQ]).astype(BF16),
            b_gates=p["b_gates"][l],
            conv_w=jnp.zeros((SUBLANES, 2 * M_WIDTH), F32).at[:M_CONV].set(p["conv_w"][l]),
            conv_b=p["conv_b"][l].reshape(1, -1),
            w_q=w_q,
            w_kv=p["w_ukv"][l].astype(BF16),
            qg=p["q_norm_w"][l].reshape(1, -1),
            kvg=p["kv_norm_w"][l].reshape(1, -1),
            w_out=p["w_out"][l].astype(BF16),
            mn=p["m_norm_w"][l].reshape(1, -1),
            ln1_g=p["ln1_g"][l].reshape(1, -1),
            ln1_b=p["ln1_b"][l].reshape(1, -1),
            wr=jnp.concatenate([w_router, jnp.zeros((d, LANES - N_EXPERTS), F32)], axis=1),
            w_gate=p["w_gate"][l].astype(BF16),
            w_up=p["w_up"][l].astype(BF16),
            w_down=p["w_down"][l].astype(BF16),
            ln2_g=p["ln2_g"][l].reshape(1, -1),
            ln2_b=p["ln2_b"][l].reshape(1, -1),
        ))
    cos_t, sin_t = _rope_tables(p["x"].shape[1], p["ctx"].shape[1])
    return dict(modtab=modtab, layers=layers, cos=cos_t, sin=sin_t)


def _layer(xs, st, l, seq, last):
    b, s, d = xs.shape
    lw, modtab = st["layers"][l], st["modtab"][l]
    qk_pre, v, o, cq, ckv, kra, krb, gcol, grow = _inproj(xs, modtab, lw["w_p"], lw["w_g"], seq)
    q, kt = _conv(qk_pre, lw["conv_w"], lw["conv_b"], seq)
    gc, gr = _gate_prep(gcol, grow, lw["b_gates"])
    hf, hb = _mlstm(q, kt, v, gc, gr, seq)
    qa, ka, va = _mla_prep(cq, ckv, kra, krb, st["cos"], st["sin"], lw["qg"], lw["kvg"], lw["w_q"], lw["w_kv"])
    a_x, a_c = _attention(qa, ka, va, seq)
    x1, h2a, afft = _outproj(hf, hb, o, a_x, a_c, xs, modtab, lw["w_out"], lw["mn"], lw["ln1_g"], lw["ln1_b"],
                             lw["wr"], seq, with_ctx=not last)
    cap_x = EC_CAPACITY * seq // N_EXPERTS
    if last:
        return _moe_set(afft[:, :, :seq], h2a, x1, modtab, lw, cap_x, 0, False)
    cap_c = EC_CAPACITY * (s - seq) // N_EXPERTS
    x2_ctx = _moe_set(afft[:, :, seq:], h2a, x1, modtab, lw, cap_c, seq, True)
    return _moe_set(afft[:, :, :seq], h2a, x1, modtab, lw, cap_x, 0, False, tail=x2_ctx)


def kernel(x, c, ctx, c_ctx, w_mod, b_mod, w_in, b_gates, conv_w, conv_b, m_norm_w, q_norm_w, kv_norm_w, w_uq, w_ukv,
           w_out, ln1_g, ln1_b, w_router, w_gate, w_up, w_down, ln2_g, ln2_b):
    p = dict(x=x, c=c, ctx=ctx, c_ctx=c_ctx, w_mod=w_mod, b_mod=b_mod, w_in=w_in, b_gates=b_gates, conv_w=conv_w,
             conv_b=conv_b, m_norm_w=m_norm_w, q_norm_w=q_norm_w, kv_norm_w=kv_norm_w, w_uq=w_uq, w_ukv=w_ukv,
             w_out=w_out, ln1_g=ln1_g, ln1_b=ln1_b, w_router=w_router, w_gate=w_gate, w_up=w_up, w_down=w_down,
             ln2_g=ln2_g, ln2_b=ln2_b)
    st = _prepare(p)
    seq = x.shape[1]
    xs = jnp.concatenate([x, ctx], axis=1)
    nl = w_in.shape[0]
    for l in range(nl):
        xs = _layer(xs, st, l, seq, last=(l == nl - 1))
    return xs
```

```python
import functools
import math

import jax
import jax.numpy as jnp
from jax import lax
from jax.experimental import pallas as pl
from jax.experimental.pallas import tpu as pltpu

F32 = jnp.float32
BF16 = jnp.bfloat16
HIGHEST = lax.Precision.HIGHEST

D_MODEL = 1024
M_HEADS = 4
M_HEAD_DIM = 128
M_WIDTH = 512
M_CONV = 5
A_HEADS = 4
A_NOPE = 128
A_ROPE = 64
A_V = 128
A_WIDTH = 512
Q_LORA = 384
KV_LORA = 256
A_SCALE = (A_NOPE + A_ROPE) ** -0.5
Q_SCALE = A_SCALE * math.log2(math.e)
ROPE_BASE = 10000.0
GRID_W = 64
N_EXPERTS = 16
EXPERT_FF = 1024
EC_CAPACITY = 2
DEPTH = 2
ALPHA = (2 * DEPTH) ** 0.25
OFF_Q, OFF_K, OFF_V, OFF_O, OFF_G = 0, 512, 1024, 1536, 2048
OFF_CQ = OFF_G + 16
OFF_CKV = OFF_CQ + Q_LORA
OFF_KR = OFF_CKV + KV_LORA
IN_COLS = OFF_KR + A_ROPE

LANES = 128
SUBLANES = 8
VMEM_LIMIT = 56 * 1024 * 1024

P_QK = 0
P_V = 1024
P_O = 1536
P_CQ = 2048
P_CKV = 2432
P_KRA = 2688
P_KRB = 2816
P_G = 2944
P_COLS = 3072

ROW_TILE = 256
MCHUNK = 256


def _cparams(sem):
    return pltpu.CompilerParams(dimension_semantics=sem, vmem_limit_bytes=VMEM_LIMIT)


def _ln(x, eps):
    mu = jnp.mean(x, axis=-1, keepdims=True)
    xc = x - mu
    var = jnp.mean(xc * xc, axis=-1, keepdims=True)
    return xc * lax.rsqrt(var + eps)


def _silu(x):
    return x * (1.0 / (1.0 + jnp.exp(-x)))


def _mod_kernel(c_ref, w_ref, b_ref, o_ref):
    c = c_ref[...]
    a = _silu(c)
    o_ref[0] = jnp.dot(a, w_ref[0], precision=HIGHEST, preferred_element_type=F32) + b_ref[0]


def _modulation(cvec, w_mod, b_mod):
    nl, d, n6 = w_mod.shape
    r = cvec.shape[0]
    tn = 1536
    return pl.pallas_call(
        _mod_kernel,
        grid=(nl, n6 // tn),
        in_specs=[
            pl.BlockSpec((r, d), lambda l, j: (0, 0)),
            pl.BlockSpec((1, d, tn), lambda l, j: (l, 0, j)),
            pl.BlockSpec((1, 1, tn), lambda l, j: (l, 0, j)),
        ],
        out_specs=pl.BlockSpec((1, r, tn), lambda l, j: (l, 0, j)),
        out_shape=jax.ShapeDtypeStruct((nl, r, n6), F32),
        compiler_params=_cparams(("arbitrary", "arbitrary")),
        name="modulation",
    )(cvec, w_mod, b_mod.reshape(nl, 1, n6))


def _inproj_kernel(x_ref, mod_ref, w_ref, wg_ref, qk_ref, v_ref, o_ref, cq_ref, ckv_ref,
                   kra_ref, krb_ref, gcol_ref, grow_ref, *, n_xblk):
    i = pl.program_id(1)
    is_ctx = i >= n_xblk
    x = x_ref[0]
    shift = jnp.where(is_ctx, mod_ref[0, 1, 0:1, :], mod_ref[0, 0, 0:1, :])
    scale = jnp.where(is_ctx, mod_ref[0, 1, 1:2, :], mod_ref[0, 0, 1:2, :])
    h = (_ln(x, 1e-6) * (1.0 + scale) + shift).astype(BF16)
    p = jnp.dot(h, w_ref[...], preferred_element_type=F32)
    qk_ref[0] = p[:, P_QK:P_V]
    v_ref[0] = p[:, P_V:P_O].astype(BF16)
    o_ref[0] = p[:, P_O:P_CQ]
    cq_ref[0] = p[:, P_CQ:P_CKV]
    ckv_ref[0] = p[:, P_CKV:P_KRA]
    kra_ref[0] = p[:, P_KRA:P_KRB]
    krb_ref[0] = p[:, P_KRB:P_G]
    gcol_ref[0] = p[:, P_G:P_COLS]
    grow_ref[0] = lax.dot_general(wg_ref[...], h, (((1,), (1,)), ((), ())), preferred_element_type=F32)


def _inproj(xs, modtab, w_p, w_g, seq):
    b, s, d = xs.shape
    tm = ROW_TILE
    widths = [(1024, F32), (512, BF16), (512, F32), (Q_LORA, F32), (KV_LORA, F32), (128, F32), (128, F32),
              (128, F32)]
    out_shape = [jax.ShapeDtypeStruct((b, s, w), dt) for w, dt in widths]
    out_specs = [pl.BlockSpec((1, tm, w), lambda bi, i: (bi, i, 0)) for w, _ in widths]
    out_shape.append(jax.ShapeDtypeStruct((b, 16, s), F32))
    out_specs.append(pl.BlockSpec((1, 16, tm), lambda bi, i: (bi, 0, i)))
    return pl.pallas_call(
        functools.partial(_inproj_kernel, n_xblk=seq // tm),
        grid=(b, s // tm),
        in_specs=[
            pl.BlockSpec((1, tm, d), lambda bi, i: (bi, i, 0)),
            pl.BlockSpec((1, 2, 6, d), lambda bi, i: (bi, 0, 0, 0)),
            pl.BlockSpec((d, P_COLS), lambda bi, i: (0, 0)),
            pl.BlockSpec((16, d), lambda bi, i: (0, 0)),
        ],
        out_specs=out_specs,
        out_shape=out_shape,
        compiler_params=_cparams(("arbitrary", "arbitrary")),
        name="ln_inproj",
    )(xs, modtab, w_p, w_g)


def _conv_kernel(main_ref, prev_ref, next_ref, w_ref, b_ref, q_ref, kt_ref, ext_ref, *, tm, n_xblk, n_blk):
    i = pl.program_id(1)
    c = main_ref.shape[-1]
    first = jnp.logical_or(i == 0, i == n_xblk)
    last = jnp.logical_or(i == n_xblk - 1, i == n_blk - 1)
    zeros8 = jnp.zeros((SUBLANES, c), F32)
    ext_ref[0:SUBLANES, :] = jnp.where(first, zeros8, prev_ref[0])
    ext_ref[SUBLANES:SUBLANES + tm, :] = main_ref[0]
    ext_ref[SUBLANES + tm:, :] = jnp.where(last, zeros8, next_ref[0])
    acc = jnp.zeros((tm, c), F32) + b_ref[...]
    for j in range(M_CONV):
        off = SUBLANES - M_CONV // 2 + j
        acc = acc + ext_ref[off:off + tm, :] * w_ref[j:j + 1, :]
    y = _silu(acc)
    q_ref[0] = (y[:, :M_WIDTH] * (M_HEAD_DIM ** -0.5)).astype(BF16)
    kt_ref[0] = jnp.transpose(y[:, M_WIDTH:]).astype(BF16)


def _conv(qk_pre, conv_w, conv_b, seq):
    b, s, c = qk_pre.shape
    tm = ROW_TILE
    r8 = tm // SUBLANES
    n_blk = s // tm
    return pl.pallas_call(
        functools.partial(_conv_kernel, tm=tm, n_xblk=seq // tm, n_blk=n_blk),
        grid=(b, n_blk),
        in_specs=[
            pl.BlockSpec((1, tm, c), lambda bi, i: (bi, i, 0)),
            pl.BlockSpec((1, SUBLANES, c), lambda bi, i: (bi, jnp.maximum(i * r8 - 1, 0), 0)),
            pl.BlockSpec((1, SUBLANES, c), lambda bi, i: (bi, jnp.minimum((i + 1) * r8, s // SUBLANES - 1), 0)),
            pl.BlockSpec((SUBLANES, c), lambda bi, i: (0, 0)),
            pl.BlockSpec((1, c), lambda bi, i: (0, 0)),
        ],
        out_specs=[
            pl.BlockSpec((1, tm, M_WIDTH), lambda bi, i: (bi, i, 0)),
            pl.BlockSpec((1, M_WIDTH, tm), lambda bi, i: (bi, 0, i)),
        ],
        out_shape=[
            jax.ShapeDtypeStruct((b, s, M_WIDTH), BF16),
            jax.ShapeDtypeStruct((b, M_WIDTH, s), BF16),
        ],
        scratch_shapes=[pltpu.VMEM((tm + 2 * SUBLANES, c), F32)],
        compiler_params=_cparams(("arbitrary", "arbitrary")),
        name="conv_silu",
    )(qk_pre, qk_pre, qk_pre, conv_w, conv_b)


def _log_sigmoid(x):
    return jnp.minimum(x, 0.0) - jnp.log1p(jnp.exp(-jnp.abs(x)))


def _gate_kernel(gcol_ref, grow_ref, bcol_ref, brow_ref, ocol_ref, orow_ref, *, lc):
    nh = M_HEADS
    r = lax.broadcasted_iota(jnp.int32, (lc, lc), 0)
    c = lax.broadcasted_iota(jnp.int32, (lc, lc), 1)
    lower = (c <= r).astype(F32)
    upper = (c >= r).astype(F32)
    g = gcol_ref[0] + bcol_ref[...]
    lf = _log_sigmoid(g)
    lane = lax.broadcasted_iota(jnp.int32, g.shape, 1)
    lf_f = jnp.where(jnp.logical_and(lane >= nh, lane < 2 * nh), lf, 0.0)
    lf_b = jnp.where(jnp.logical_and(lane >= 3 * nh, lane < 4 * nh), lf, 0.0)
    cs_f = jnp.dot(lower, lf_f, precision=HIGHEST, preferred_element_type=F32)
    cs_b = jnp.dot(upper, lf_b, precision=HIGHEST, preferred_element_type=F32)
    ocol_ref[0] = cs_f + cs_b
    gr = grow_ref[0] + brow_ref[...]
    lfr = _log_sigmoid(gr)
    rs_f = jnp.dot(lfr, upper, precision=HIGHEST, preferred_element_type=F32)
    rs_b = jnp.dot(lfr, lower, precision=HIGHEST, preferred_element_type=F32)
    orow_ref[0, 0:nh, :] = gr[0:nh] - rs_f[nh:2 * nh]
    orow_ref[0, nh:2 * nh, :] = gr[2 * nh:3 * nh] - rs_b[3 * nh:4 * nh]
    orow_ref[0, 2 * nh:3 * nh, :] = rs_f[nh:2 * nh]
    orow_ref[0, 3 * nh:4 * nh, :] = rs_b[3 * nh:4 * nh]


def _gate_prep(gcol, grow, b_gates):
    b, s, _ = gcol.shape
    lc = MCHUNK
    bcol = jnp.zeros((1, LANES), F32).at[0, :16].set(b_gates)
    brow = jnp.broadcast_to(b_gates[:, None], (16, LANES)).astype(F32)
    return pl.pallas_call(
        functools.partial(_gate_kernel, lc=lc),
        grid=(b, s // lc),
        in_specs=[
            pl.BlockSpec((1, lc, LANES), lambda bi, i: (bi, i, 0)),
            pl.BlockSpec((1, 16, lc), lambda bi, i: (bi, 0, i)),
            pl.BlockSpec((1, LANES), lambda bi, i: (0, 0)),
            pl.BlockSpec((16, lc), lambda bi, i: (0, 0)),
        ],
        out_specs=[
            pl.BlockSpec((1, lc, LANES), lambda bi, i: (bi, i, 0)),
            pl.BlockSpec((1, 16, lc), lambda bi, i: (bi, 0, i)),
        ],
        out_shape=[
            jax.ShapeDtypeStruct((b, s, LANES), F32),
            jax.ShapeDtypeStruct((b, 16, s), F32),
        ],
        compiler_params=_cparams(("arbitrary", "arbitrary")),
        name="mlstm_gates",
    )(gcol, grow, bcol, jnp.broadcast_to(b_gates[:, None], (16, lc)).astype(F32))


def _mlstm_kernel(qf_ref, ktf_ref, vf_ref, gcf_ref, grf_ref, qb_ref, ktb_ref, vb_ref, gcb_ref, grb_ref,
                  hf_ref, hb_ref, c_ref, m_ref, *, lc):
    step = pl.program_id(1)
    nh, dh = M_HEADS, M_HEAD_DIM

    @pl.when(step == 0)
    def _():
        c_ref[...] = jnp.zeros(c_ref.shape, F32)
        m_ref[...] = jnp.zeros(m_ref.shape, F32)

    row_i = lax.broadcasted_iota(jnp.int32, (lc, lc), 0)
    col_j = lax.broadcasted_iota(jnp.int32, (lc, lc), 1)
    lane = lax.broadcasted_iota(jnp.int32, (lc, LANES), 1)
    ones_col = jnp.where(lane == 0, 1.0, 0.0).astype(BF16)
    dirs = ((qf_ref, ktf_ref, vf_ref, gcf_ref, grf_ref, hf_ref, col_j <= row_i),
            (qb_ref, ktb_ref, vb_ref, gcb_ref, grb_ref, hb_ref, col_j >= row_i))
    for d, (q_ref, kt_ref, v_ref, gc_ref, gr_ref, h_ref, mask) in enumerate(dirs):
        for hd in range(nh):
            sl = slice(hd * dh, (hd + 1) * dh)
            q = q_ref[0, :, sl]
            kt = kt_ref[0, sl, :]
            v = v_ref[0, :, sl]
            gi = (1 + 2 * d) * nh + hd
            b_col = gc_ref[0, :, gi:gi + 1]
            a_row = gr_ref[0, d * nh + hd:d * nh + hd + 1, :]
            b_row = gr_ref[0, (2 + d) * nh + hd:(2 + d) * nh + hd + 1, :]
            b_last = b_row[:, lc - 1:lc] if d == 0 else b_row[:, 0:1]
            sidx = d * nh + hd
            m_old = m_ref[sidx, :, 0:1]
            c_old = c_ref[sidx]
            d_log = jnp.where(mask, b_col + a_row, -jnp.inf)
            m_inter = b_col + m_old
            m_t = jnp.maximum(jnp.max(d_log, axis=1, keepdims=True), m_inter)
            s_mat = jnp.dot(q, kt, preferred_element_type=F32) * jnp.exp(d_log - m_t)
            w_inter = jnp.exp(m_inter - m_t)
            v_ext = jnp.concatenate([v, ones_col], axis=1)
            num_ext = (jnp.dot(s_mat.astype(BF16), v_ext, preferred_element_type=F32)
                       + w_inter * jnp.dot(q, c_old.astype(BF16), preferred_element_type=F32))
            den = num_ext[:, dh:dh + 1]
            h_ref[0, :, sl] = num_ext[:, :dh] / jnp.maximum(jnp.abs(den), jnp.exp(-m_t))
            g_row = b_last + a_row
            m_new = jnp.maximum(b_last + m_old, jnp.max(g_row, axis=1, keepdims=True))
            w_s = jnp.exp(g_row - m_new)
            decay = jnp.exp(b_last + m_old - m_new)
            kw = (kt.astype(F32) * w_s).astype(BF16)
            c_ref[sidx] = decay * c_old + jnp.dot(kw, v_ext, preferred_element_type=F32)
            m_ref[sidx] = jnp.broadcast_to(m_new, (1, LANES))


def _mlstm(q, kt, v, gcol, grow, seq):
    b, s, w = q.shape
    lc = MCHUNK
    nc, ncx = s // lc, seq // lc
    ncc = nc - ncx

    def fmap(c):
        return jnp.where(c < ncc, ncx + c, c - ncc)

    def bmap(c):
        return jnp.where(c < ncc, nc - 1 - c, ncx - 1 - (c - ncc))

    def specs(cmap):
        return [
            pl.BlockSpec((1, lc, w), lambda bi, c: (bi, cmap(c), 0)),
            pl.BlockSpec((1, w, lc), lambda bi, c: (bi, 0, cmap(c))),
            pl.BlockSpec((1, lc, w), lambda bi, c: (bi, cmap(c), 0)),
            pl.BlockSpec((1, lc, LANES), lambda bi, c: (bi, cmap(c), 0)),
            pl.BlockSpec((1, 16, lc), lambda bi, c: (bi, 0, cmap(c))),
        ]

    return pl.pallas_call(
        functools.partial(_mlstm_kernel, lc=lc),
        grid=(b, nc),
        in_specs=specs(fmap) + specs(bmap),
        out_specs=[
            pl.BlockSpec((1, lc, w), lambda bi, c: (bi, fmap(c), 0)),
            pl.BlockSpec((1, lc, w), lambda bi, c: (bi, bmap(c), 0)),
        ],
        out_shape=[jax.ShapeDtypeStruct((b, s, w), F32), jax.ShapeDtypeStruct((b, s, w), F32)],
        scratch_shapes=[
            pltpu.VMEM((2 * M_HEADS, M_HEAD_DIM, 2 * M_HEAD_DIM), F32),
            pltpu.VMEM((2 * M_HEADS, 1, LANES), F32),
        ],
        compiler_params=_cparams(("arbitrary", "arbitrary")),
        name="mlstm",
    )(q, kt, v, gcol, grow, q, kt, v, gcol, grow)


def _rms(x, g, eps=1e-6):
    return x * lax.rsqrt(jnp.mean(x * x, axis=-1, keepdims=True) + eps) * g


def _mla_prep_kernel(cq_ref, ckv_ref, kra_ref, krb_ref, cos_ref, sin_ref, qg_ref, kvg_ref, wq_ref, wkv_ref,
                     q_ref, k_ref, v_ref):
    nh, dn = A_HEADS, A_NOPE
    cos, sin = cos_ref[...], sin_ref[...]
    cqn = _rms(cq_ref[0], qg_ref[...]).astype(BF16)
    qa = jnp.dot(cqn, wq_ref[...], preferred_element_type=F32)
    ckvn = _rms(ckv_ref[0], kvg_ref[...]).astype(BF16)
    kva = jnp.dot(ckvn, wkv_ref[...], preferred_element_type=F32)
    kr = kra_ref[0] * cos + krb_ref[0] * sin
    lane = lax.broadcasted_iota(jnp.int32, kr.shape, 1)
    ones_col = jnp.where(lane == 0, 1.0, 0.0)
    for h in range(nh):
        qr = qa[:, (nh + h) * dn:(nh + h + 1) * dn] * cos + qa[:, (2 * nh + h) * dn:(2 * nh + h + 1) * dn] * sin
        q_ref[0, h] = (jnp.concatenate([qa[:, h * dn:(h + 1) * dn], qr], axis=1) * Q_SCALE).astype(BF16)
        k_ref[0, h] = jnp.concatenate([kva[:, 2 * h * dn:(2 * h + 1) * dn], kr], axis=1).astype(BF16)
        v_ref[0, h] = jnp.concatenate([kva[:, (2 * h + 1) * dn:(2 * h + 2) * dn], ones_col], axis=1).astype(BF16)


def _mla_prep(cq, ckv, kra, krb, cos_t, sin_t, qg, kvg, w_q, w_kv):
    b, s, _ = cq.shape
    tm = ROW_TILE
    nh = A_HEADS
    row = lambda w: pl.BlockSpec((1, tm, w), lambda bi, i: (bi, i, 0))
    const = lambda a: pl.BlockSpec(a.shape, lambda bi, i: (0, 0))
    head = lambda w: pl.BlockSpec((1, nh, tm, w), lambda bi, i: (bi, 0, i, 0))
    return pl.pallas_call(
        _mla_prep_kernel,
        grid=(b, s // tm),
        in_specs=[row(Q_LORA), row(KV_LORA), row(LANES), row(LANES),
                  pl.BlockSpec((tm, LANES), lambda bi, i: (i, 0)), pl.BlockSpec((tm, LANES), lambda bi, i: (i, 0)),
                  const(qg), const(kvg), const(w_q), const(w_kv)],
        out_specs=[head(2 * LANES), head(2 * LANES), head(2 * LANES)],
        out_shape=[jax.ShapeDtypeStruct((b, nh, s, 2 * LANES), BF16)] * 3,
        compiler_params=_cparams(("arbitrary", "arbitrary")),
        name="mla_prep",
    )(cq, ckv, kra, krb, cos_t, sin_t, qg, kvg, w_q, w_kv)


ATT_TQ = 1024
ATT_TK = 768


def _attn_kernel(q_ref, k_ref, v_ref, o_ref, *, tk, nk):
    q = q_ref[0, 0]
    tq = q.shape[0]
    m = jnp.full((tq, 1), -jnp.inf, F32)
    acc = jnp.zeros((tq, 2 * A_V), F32)
    for j in range(nk):
        k = k_ref[0, 0, j * tk:(j + 1) * tk, :]
        v = v_ref[0, 0, j * tk:(j + 1) * tk, :]
        s = lax.dot_general(q, k, (((1,), (1,)), ((), ())), preferred_element_type=F32)
        m_new = jnp.maximum(m, jnp.max(s, axis=1, keepdims=True))
        p = jnp.exp2(s - m_new)
        acc = jnp.exp2(m - m_new) * acc + jnp.dot(p.astype(BF16), v, preferred_element_type=F32)
        m = m_new
    o_ref[0] = (acc[:, :A_V] / acc[:, A_V:A_V + 1]).astype(BF16)


def _attention(q, k, v, seq):
    b, nh, s, dq = q.shape
    tq, tk = ATT_TQ, ATT_TK
    ctx_len = s - seq
    cblk = seq // ctx_len
    a_x = pl.pallas_call(
        functools.partial(_attn_kernel, tk=tk, nk=s // tk),
        grid=(b, nh, seq // tq),
        in_specs=[
            pl.BlockSpec((1, 1, tq, dq), lambda bi, h, i: (bi, h, i, 0)),
            pl.BlockSpec((1, 1, s, dq), lambda bi, h, i: (bi, h, 0, 0)),
            pl.BlockSpec((1, 1, s, dq), lambda bi, h, i: (bi, h, 0, 0)),
        ],
        out_specs=pl.BlockSpec((1, tq, A_V), lambda bi, h, i: (bi, i, h)),
        out_shape=jax.ShapeDtypeStruct((b, seq, nh * A_V), BF16),
        compiler_params=_cparams(("arbitrary", "arbitrary", "arbitrary")),
        name="mla_attention",
    )(q, k, v)
    ctx_spec = pl.BlockSpec((1, 1, ctx_len, dq), lambda bi, h: (bi, h, cblk, 0))
    a_c = pl.pallas_call(
        functools.partial(_attn_kernel, tk=ctx_len, nk=1),
        grid=(b, nh),
        in_specs=[ctx_spec, ctx_spec, ctx_spec],
        out_specs=pl.BlockSpec((1, ctx_len, A_V), lambda bi, h: (bi, 0, h)),
        out_shape=jax.ShapeDtypeStruct((b, ctx_len, nh * A_V), BF16),
        compiler_params=_cparams(("arbitrary", "arbitrary")),
        name="mla_attention_ctx",
    )(q, k, v)
    return a_x, a_c


def _outproj_kernel(hf_ref, hb_ref, o_ref, ax_ref, ac_ref, x_ref, mod_ref, wout_ref, mn_ref, g_ref, b_ref, wr_ref,
                    x1_ref, h2_ref, afft_ref, *, n_xblk):
    i = pl.program_id(1)
    is_ctx = i >= n_xblk
    mod = lambda k: jnp.where(is_ctx, mod_ref[0, 1, k:k + 1, :], mod_ref[0, 0, k:k + 1, :])
    hsum = hf_ref[0] + hb_ref[0]
    dh = M_HEAD_DIM
    hn = jnp.concatenate([_ln(hsum[:, h * dh:(h + 1) * dh], 1e-6) for h in range(M_HEADS)], axis=1)
    o = o_ref[0]
    m_out = hn * mn_ref[...] * (1.0 / (1.0 + jnp.exp(-o)))
    cat = jnp.concatenate([m_out.astype(BF16), jnp.where(is_ctx, ac_ref[0], ax_ref[0])], axis=1)
    y = jnp.dot(cat, wout_ref[...], preferred_element_type=F32)
    x1 = _ln(ALPHA * x_ref[0] + mod(2) * y, 1e-5) * g_ref[...] + b_ref[...]
    x1_ref[0] = x1
    h2 = _ln(x1, 1e-6) * (1.0 + mod(4)) + mod(3)
    logits = jnp.dot(h2, wr_ref[...], precision=HIGHEST, preferred_element_type=F32)
    lane = lax.broadcasted_iota(jnp.int32, logits.shape, 1)
    logits = jnp.where(lane < N_EXPERTS, logits, -jnp.inf)
    e = jnp.exp(logits - jnp.max(logits, axis=1, keepdims=True))
    aff = e / jnp.sum(e, axis=1, keepdims=True)
    h2_ref[0] = jnp.concatenate([h2, aff], axis=1)
    afft_ref[0] = jnp.transpose(aff)[:N_EXPERTS, :]


def _outproj(hf, hb, o, a_x, a_c, xs, modtab, w_out, mn, g, bb, wr, seq, with_ctx):
    b, s, d = xs.shape
    tm = ROW_TILE
    n_xblk = seq // tm
    if not with_ctx:
        s = seq
    row = lambda w: pl.BlockSpec((1, tm, w), lambda bi, i: (bi, i, 0))
    const = lambda a: pl.BlockSpec(a.shape, lambda bi, i: (0, 0))
    return pl.pallas_call(
        functools.partial(_outproj_kernel, n_xblk=n_xblk),
        grid=(b, s // tm),
        in_specs=[row(M_WIDTH), row(M_WIDTH), row(M_WIDTH),
                  pl.BlockSpec((1, tm, A_WIDTH), lambda bi, i: (bi, jnp.minimum(i, n_xblk - 1), 0)),
                  pl.BlockSpec((1, tm, A_WIDTH), lambda bi, i: (bi, jnp.maximum(i - n_xblk, 0), 0)),
                  row(d),
                  pl.BlockSpec((1, 2, 6, d), lambda bi, i: (bi, 0, 0, 0)),
                  const(w_out), const(mn), const(g), const(bb), const(wr)],
        out_specs=[row(d), row(d + LANES), pl.BlockSpec((1, N_EXPERTS, tm), lambda bi, i: (bi, 0, i))],
        out_shape=[jax.ShapeDtypeStruct((b, s, d), F32), jax.ShapeDtypeStruct((b, s, d + LANES), F32),
                   jax.ShapeDtypeStruct((b, N_EXPERTS, s), F32)],
        compiler_params=_cparams(("arbitrary", "arbitrary")),
        name="merge_outproj_norm_router",
    )(hf, hb, o, a_x, a_c, xs, modtab, w_out, mn, g, bb, wr)


def _seg(n):
    return -(-n // 1024) * 1024


def _route_kernel(aff_ref, idx_ref, sel_ref, cnt_ref, *, cap, nrow):
    ne = N_EXPERTS
    aff = aff_ref[0]
    bits = pltpu.bitcast(aff, jnp.int32)

    def count(mask):
        return jnp.sum(jnp.sum(mask.astype(F32), axis=1, keepdims=True), axis=2, keepdims=True)

    def bit_step(i, thr):
        cand = jnp.bitwise_or(thr, lax.shift_left(jnp.int32(1), 30 - i))
        return jnp.where(count(bits >= cand) >= cap, cand, thr)

    thr = lax.fori_loop(0, 31, bit_step, jnp.zeros((ne, 1, 1), jnp.int32))
    gt = bits > thr
    eq = bits == thr
    need = cap - count(gt)

    li = lax.broadcasted_iota(jnp.int32, (LANES, LANES), 0)
    lj = lax.broadcasted_iota(jnp.int32, (LANES, LANES), 1)
    tri = (li <= lj).astype(BF16)
    nr = ne * nrow
    shift = int(math.log2(nrow))
    bi = lax.broadcasted_iota(jnp.int32, (nr, nr), 0)
    bj = lax.broadcasted_iota(jnp.int32, (nr, nr), 1)
    same = lax.shift_right_logical(bi, shift) == lax.shift_right_logical(bj, shift)
    blk = jnp.logical_and(same, bj < bi).astype(BF16)

    def prefix(mask):
        m = mask.astype(BF16).reshape(nr, LANES)
        local = jnp.dot(m, tri, preferred_element_type=F32)
        tot = jnp.broadcast_to(local[:, LANES - 1:LANES], (nr, LANES)).astype(BF16)
        offs = jnp.dot(blk, tot, preferred_element_type=F32)
        return local, offs

    eq_local, eq_offs = prefix(eq)
    eq_f = eq.astype(F32)
    eq_rank = (eq_local + eq_offs).reshape(ne, nrow, LANES) - eq_f
    sel = jnp.logical_or(gt, jnp.logical_and(eq, eq_rank < need))
    local, offs = prefix(sel)
    cnt = (local + offs).reshape(ne, nrow, LANES)
    sel_ref[0] = sel.astype(BF16)
    cnt_ref[0] = cnt.astype(jnp.int32)

    rc = jnp.max(cnt, axis=2)
    rtot = jnp.sum(sel.astype(F32), axis=2)
    rc_excl = rc - rtot
    local3 = local.reshape(ne, nrow, LANES)
    p_r = lax.broadcasted_iota(jnp.int32, (cap, nrow), 0).astype(F32)
    r_r = lax.broadcasted_iota(jnp.int32, (cap, nrow), 1).astype(F32)
    p_l = lax.broadcasted_iota(jnp.int32, (cap, LANES), 0).astype(F32)
    lane = lax.broadcasted_iota(jnp.int32, (cap, LANES), 1)
    out = jnp.zeros((cap, LANES), jnp.int32)
    for e in range(ne):
        rowsel = jnp.sum((rc[e:e + 1, :] <= p_r).astype(F32), axis=1, keepdims=True)
        onehot = (r_r == rowsel)
        in_row = jnp.dot(onehot.astype(BF16), local3[e].astype(BF16), preferred_element_type=F32)
        before = jnp.sum(jnp.where(onehot, rc_excl[e:e + 1, :], 0.0), axis=1, keepdims=True)
        lane_of = jnp.sum((in_row + before <= p_l).astype(F32), axis=1, keepdims=True)
        tok = (rowsel * LANES + lane_of).astype(jnp.int32)
        out = jnp.where(lane == e, tok, out)
    idx_ref[0] = out


def _route(aff4, cap):
    b, ne, nrow, _ = aff4.shape
    return pl.pallas_call(
        functools.partial(_route_kernel, cap=cap, nrow=nrow),
        grid=(b,),
        in_specs=[pl.BlockSpec((1, ne, nrow, LANES), lambda bi: (bi, 0, 0, 0))],
        out_specs=[pl.BlockSpec((1, cap, LANES), lambda bi: (bi, 0, 0)),
                   pl.BlockSpec((1, ne, nrow, LANES), lambda bi: (bi, 0, 0, 0)),
                   pl.BlockSpec((1, ne, nrow, LANES), lambda bi: (bi, 0, 0, 0))],
        out_shape=[jax.ShapeDtypeStruct((b, cap, LANES), jnp.int32),
                   jax.ShapeDtypeStruct((b, ne, nrow, LANES), BF16),
                   jax.ShapeDtypeStruct((b, ne, nrow, LANES), jnp.int32)],
        compiler_params=_cparams(("arbitrary",)),
        name="ec_route",
    )(aff4)


def _expert_kernel(idx_hbm, h_hbm, wg_ref, wu_ref, wd_ref, y_ref, idx0, idx1, xin0, xin1, sem_i, sem_g, *,
                   cap, s_rows, row_off, nb):
    e = pl.program_id(0)
    b = pl.program_id(1)
    d = D_MODEL
    seg = idx0.shape[0]
    step = e * nb + b
    nsteps = N_EXPERTS * nb
    idx_bufs, x_bufs = (idx0, idx1), (xin0, xin1)

    def idx_copy(st, sl):
        eb = lax.rem(st, nb) * N_EXPERTS + st // nb
        return pltpu.make_async_copy(idx_hbm.at[pl.ds(pl.multiple_of(eb * seg, seg), seg)], idx_bufs[sl],
                                     sem_i.at[sl])

    def row_copy(st, sl, p):
        base = lax.rem(st, nb) * s_rows + row_off
        return pltpu.make_async_copy(h_hbm.at[pl.ds(base + idx_bufs[sl][p], 1)], x_bufs[sl].at[pl.ds(p, 1)],
                                     sem_g.at[sl])

    def rows_done(sl):
        return pltpu.make_async_copy(h_hbm.at[pl.ds(0, cap)], x_bufs[sl], sem_g.at[sl])

    @pl.when(step == 0)
    def _():
        idx_copy(0, 0).start()
        idx_copy(0, 0).wait()

        def body(p, c):
            row_copy(0, 0, p).start()
            return c

        lax.fori_loop(0, cap, body, 0)
        idx_copy(1, 1).start()

    def run(sl):
        nxt = lax.rem(step + 1, nsteps)
        rows_done(sl).wait()
        idx_copy(nxt, 1 - sl).wait()
        for p in range(cap):
            row_copy(nxt, 1 - sl, p).start()
        idx_copy(lax.rem(step + 2, nsteps), sl).start()
        x = x_bufs[sl][:, :d].astype(BF16)
        aff = x_bufs[sl][:, d:]
        lane = lax.broadcasted_iota(jnp.int32, aff.shape, 1)
        gate = jnp.sum(jnp.where(lane == e, aff, 0.0), axis=1, keepdims=True)
        hg = jnp.dot(x, wg_ref[0], preferred_element_type=F32)
        hu = jnp.dot(x, wu_ref[0], preferred_element_type=F32)
        act = (_silu(hg) * hu).astype(BF16)
        y_ref[0] = jnp.dot(act, wd_ref[0], preferred_element_type=F32) * gate

        @pl.when(step == nsteps - 1)
        def _():
            rows_done(1 - sl).wait()
            idx_copy(0, sl).wait()

    for sl in range(2):
        pl.when(lax.rem(step, 2) == sl)(functools.partial(run, sl))


def _experts(idx_tab, h2a, w_gate, w_up, w_down, cap, row_off):
    b, s, dw = h2a.shape
    ne, d, ff = w_gate.shape
    wspec = lambda r, c: pl.BlockSpec((1, r, c), lambda e, bi: (e, 0, 0))
    return pl.pallas_call(
        functools.partial(_expert_kernel, cap=cap, s_rows=s, row_off=row_off, nb=b),
        grid=(ne, b),
        in_specs=[pl.BlockSpec(memory_space=pl.ANY), pl.BlockSpec(memory_space=pl.ANY),
                  wspec(d, ff), wspec(d, ff), wspec(ff, d)],
        out_specs=pl.BlockSpec((1, cap, d), lambda e, bi: (bi, e, 0)),
        out_shape=jax.ShapeDtypeStruct((b, ne * cap, d), F32),
        scratch_shapes=[pltpu.SMEM((_seg(cap),), jnp.int32), pltpu.SMEM((_seg(cap),), jnp.int32),
                        pltpu.VMEM((cap, dw), F32), pltpu.VMEM((cap, dw), F32),
                        pltpu.SemaphoreType.DMA((2,)), pltpu.SemaphoreType.DMA((2,))],
        compiler_params=_cparams(("arbitrary", "arbitrary")),
        name="ec_experts",
    )(idx_tab, h2a.reshape(b * s, dw), w_gate, w_up, w_down)


COMBINE_TB = 256


def _combine_kernel(idx_hbm, bnd_hbm, y_hbm, sel_ref, x1_ref, mod_ref, g_ref, b_ref, *rest, cap, tb, is_ctx, nblk,
                    has_tail):
    if has_tail:
        tail_ref, o_ref, idx_smem, bnd_smem, buf_ref, sem_i, sem_g = rest
    else:
        o_ref, idx_smem, bnd_smem, buf_ref, sem_i, sem_g = rest
    b = pl.program_id(0)
    j = pl.program_id(1)
    ne = N_EXPERTS
    ybase = b * ne * cap

    def bounds(e, blk):
        return bnd_smem[e * (nblk + 1) + blk], bnd_smem[e * (nblk + 1) + blk + 1]

    def issue(blk, sl):
        for e in range(ne):
            lo, hi = bounds(e, blk)

            def body(p, c, e=e):
                tl = idx_smem[e * _seg(cap) + p] - blk * tb
                pltpu.make_async_copy(y_hbm.at[pl.ds(ybase + e * cap + p, 1)],
                                      buf_ref.at[sl, pl.ds(e * tb + tl, 1)], sem_g.at[sl]).start()
                return c

            lax.fori_loop(lo, hi, body, 0)

    def drain(blk, sl):
        n = bounds(0, blk)[1] - bounds(0, blk)[0]
        for e in range(1, ne):
            lo, hi = bounds(e, blk)
            n = n + hi - lo
        for bit in range((ne * tb).bit_length()):
            @pl.when(jnp.bitwise_and(lax.shift_right_logical(n, bit), 1) == 1)
            def _(bit=bit):
                rows = 1 << bit
                pltpu.make_async_copy(y_hbm.at[pl.ds(0, rows)], buf_ref.at[sl, pl.ds(0, rows)], sem_g.at[sl]).wait()

    @pl.when(jnp.logical_and(b == 0, j == 0))
    def _():
        buf_ref[...] = jnp.zeros(buf_ref.shape, F32)

    @pl.when(j == 0)
    def _():
        n_i, n_b = idx_smem.shape[0], bnd_smem.shape[0]
        c1 = pltpu.make_async_copy(idx_hbm.at[pl.ds(pl.multiple_of(b * n_i, n_i), n_i)], idx_smem, sem_i.at[0])
        c2 = pltpu.make_async_copy(bnd_hbm.at[pl.ds(pl.multiple_of(b * n_b, n_b), n_b)], bnd_smem, sem_i.at[1])
        c1.start()
        c2.start()
        c1.wait()
        c2.wait()
        issue(0, 0)

    @pl.when(j < nblk)
    def _():
        slot = lax.rem(j, 2)

        @pl.when(j + 1 < nblk)
        def _():
            issue(j + 1, 1 - slot)

        drain(j, slot)
        ri = lax.broadcasted_iota(jnp.int32, (tb, tb), 0)
        ci = lax.broadcasted_iota(jnp.int32, (tb, tb), 1)
        eye = (ri == ci).astype(BF16)
        mask_t = lax.dot_general(eye, sel_ref[0], (((1,), (1,)), ((), ())), preferred_element_type=F32)
        moe = jnp.zeros((tb, D_MODEL), F32)
        for e in range(ne):
            moe = moe + jnp.where(mask_t[:, e:e + 1] > 0.5, buf_ref[slot, e * tb:(e + 1) * tb, :], 0.0)
        k = 1 if is_ctx else 0
        o_ref[0] = _ln(ALPHA * x1_ref[0] + mod_ref[0, k, 5:6, :] * moe, 1e-5) * g_ref[...] + b_ref[...]

    if has_tail:
        @pl.when(j >= nblk)
        def _():
            o_ref[0] = tail_ref[0]


def _combine(idx_flat, bnd_flat, y_all, sel, x1, modtab, g, bb, cap, is_ctx, blk_off, tail=None):
    b, ne, n = sel.shape
    d = x1.shape[-1]
    tb = COMBINE_TB
    nblk = n // tb
    ntail = 0 if tail is None else tail.shape[1] // tb
    last = nblk - 1
    in_specs = [pl.BlockSpec(memory_space=pl.ANY), pl.BlockSpec(memory_space=pl.ANY), pl.BlockSpec(memory_space=pl.ANY),
                pl.BlockSpec((1, ne, tb), lambda bi, j: (bi, 0, jnp.minimum(j, last))),
                pl.BlockSpec((1, tb, d), lambda bi, j: (bi, jnp.minimum(j, last) + blk_off, 0)),
                pl.BlockSpec((1, 2, 6, d), lambda bi, j: (bi, 0, 0, 0)),
                pl.BlockSpec((1, d), lambda bi, j: (0, 0)), pl.BlockSpec((1, d), lambda bi, j: (0, 0))]
    args = [idx_flat, bnd_flat, y_all.reshape(b * ne * cap, d), sel, x1, modtab, g, bb]
    if tail is not None:
        in_specs.append(pl.BlockSpec((1, tb, d), lambda bi, j: (bi, jnp.maximum(j - nblk, 0), 0)))
        args.append(tail)
    return pl.pallas_call(
        functools.partial(_combine_kernel, cap=cap, tb=tb, is_ctx=is_ctx, nblk=nblk, has_tail=tail is not None),
        grid=(b, nblk + ntail),
        in_specs=in_specs,
        out_specs=pl.BlockSpec((1, tb, d), lambda bi, j: (bi, j, 0)),
        out_shape=jax.ShapeDtypeStruct((b, (nblk + ntail) * tb, d), F32),
        scratch_shapes=[pltpu.SMEM((ne * _seg(cap),), jnp.int32), pltpu.SMEM((_seg(ne * (nblk + 1)),), jnp.int32),
                        pltpu.VMEM((2, ne * tb, d), F32), pltpu.SemaphoreType.DMA((2,)),
                        pltpu.SemaphoreType.DMA((2,))],
        compiler_params=_cparams(("arbitrary", "arbitrary")),
        name="ec_combine_norm",
    )(*args)


def _moe_set(afft_set, h2a, x1, modtab, lw, cap, row_off, is_ctx, tail=None):
    b, ne, n = afft_set.shape
    n_pad = max(n, 16 * LANES)
    aff4 = jnp.pad(afft_set, ((0, 0), (0, 0), (0, n_pad - n)), constant_values=-1.0).reshape(b, ne, n_pad // LANES, LANES)
    idx, sel, cnt = _route(aff4, cap)
    idx_t = jnp.transpose(idx[:, :, :ne], (0, 2, 1))
    idx_flat = jnp.pad(idx_t, ((0, 0), (0, 0), (0, _seg(cap) - cap))).reshape(-1)
    nblk = n // COMBINE_TB
    rpb = COMBINE_TB // LANES
    ends = cnt[:, :, rpb - 1::rpb, LANES - 1][:, :, :nblk]
    bnd = jnp.concatenate([jnp.zeros((b, ne, 1), jnp.int32), ends], axis=2).reshape(b, -1)
    bnd_flat = jnp.pad(bnd, ((0, 0), (0, _seg(bnd.shape[1]) - bnd.shape[1]))).reshape(-1)
    y_all = _experts(idx_flat, h2a, lw["w_gate"], lw["w_up"], lw["w_down"], cap, row_off)
    sel_rows = sel.reshape(b, ne, n_pad)[:, :, :n]
    return _combine(idx_flat, bnd_flat, y_all, sel_rows, x1, modtab, lw["ln2_g"], lw["ln2_b"], cap, is_ctx,
                    row_off // COMBINE_TB, tail)


def _rope_tables(seq, ctx_len):
    rows = seq // GRID_W
    half = A_ROPE // 2
    inv = ROPE_BASE ** (-jnp.arange(0, half, 2, dtype=F32) / half)
    row = jnp.repeat(jnp.arange(rows, dtype=F32), GRID_W)
    col = jnp.tile(jnp.arange(GRID_W, dtype=F32), rows)
    ang = jnp.concatenate([row[:, None] * inv, col[:, None] * inv], axis=-1)
    cos, sin = jnp.cos(ang), jnp.sin(ang)
    pad = jnp.zeros((seq, LANES - A_ROPE), F32)
    cos_x = jnp.concatenate([cos, cos, pad], axis=1)
    sin_x = jnp.concatenate([sin, sin, pad], axis=1)
    cos_c = jnp.concatenate([jnp.ones((ctx_len, A_ROPE), F32), jnp.zeros((ctx_len, LANES - A_ROPE), F32)], axis=1)
    sin_c = jnp.zeros((ctx_len, LANES), F32)
    return jnp.concatenate([cos_x, cos_c], axis=0), jnp.concatenate([sin_x, sin_c], axis=0)


def _rot(w):
    half = A_ROPE // 2
    return jnp.concatenate([-w[..., half:], w[..., :half]], axis=-1)


def _prepare(p):
    c, c_ctx = p["c"], p["c_ctx"]
    b, d = c.shape
    nl = p["w_mod"].shape[0]
    rows = -(-(b + 1) // SUBLANES) * SUBLANES
    cvec = jnp.zeros((rows, d), F32).at[:b].set(c).at[b].set(c_ctx)
    mods = _modulation(cvec, p["w_mod"], p["b_mod"]).reshape(nl, rows, 6, d)
    mx = mods[:, :b]
    mc = jnp.broadcast_to(mods[:, b:b + 1], (nl, b, 6, d))
    modtab = jnp.stack([mx, mc], axis=2)
    layers = []
    for l in range(nl):
        w_in = p["w_in"][l]
        w_kr = w_in[:, OFF_KR:IN_COLS]
        z64 = jnp.zeros((d, LANES - A_ROPE), F32)
        w_p = jnp.concatenate([
            w_in[:, OFF_Q:OFF_G], w_in[:, OFF_CQ:OFF_KR], w_kr, z64, _rot(w_kr), z64,
            w_in[:, OFF_G:OFF_CQ], jnp.zeros((d, LANES - 16), F32)], axis=1).astype(BF16)
        w_uq = p["w_uq"][l].reshape(Q_LORA, A_HEADS, A_NOPE + A_ROPE)
        zq = jnp.zeros((Q_LORA, A_HEADS, LANES - A_ROPE), F32)
        w_q = jnp.concatenate([
            w_uq[:, :, :A_NOPE].reshape(Q_LORA, -1),
            jnp.concatenate([w_uq[:, :, A_NOPE:], zq], axis=2).reshape(Q_LORA, -1),
            jnp.concatenate([_rot(w_uq[:, :, A_NOPE:]), zq], axis=2).reshape(Q_LORA, -1)], axis=1).astype(BF16)
        w_router = p["w_router"][l]
        layers.append(dict(
            w_p=w_p,
            w_g=jnp.transpose(w_in[:, OFF_G:OFF_CQ]).astype(BF16),
            b_gates=p["b_gates"][l],
            conv_w=jnp.zeros((SUBLANES, 2 * M_WIDTH), F32).at[:M_CONV].set(p["conv_w"][l]),
            conv_b=p["conv_b"][l].reshape(1, -1),
            w_q=w_q,
            w_kv=p["w_ukv"][l].astype(BF16),
            qg=p["q_norm_w"][l].reshape(1, -1),
            kvg=p["kv_norm_w"][l].reshape(1, -1),
            w_out=p["w_out"][l].astype(BF16),
            mn=p["m_norm_w"][l].reshape(1, -1),
            ln1_g=p["ln1_g"][l].reshape(1, -1),
            ln1_b=p["ln1_b"][l].reshape(1, -1),
            wr=jnp.concatenate([w_router, jnp.zeros((d, LANES - N_EXPERTS), F32)], axis=1),
            w_gate=p["w_gate"][l].astype(BF16),
            w_up=p["w_up"][l].astype(BF16),
            w_down=p["w_down"][l].astype(BF16),
            ln2_g=p["ln2_g"][l].reshape(1, -1),
            ln2_b=p["ln2_b"][l].reshape(1, -1),
        ))
    cos_t, sin_t = _rope_tables(p["x"].shape[1], p["ctx"].shape[1])
    return dict(modtab=modtab, layers=layers, cos=cos_t, sin=sin_t)


def _layer(xs, st, l, seq, last):
    b, s, d = xs.shape
    lw, modtab = st["layers"][l], st["modtab"][l]
    qk_pre, v, o, cq, ckv, kra, krb, gcol, grow = _inproj(xs, modtab, lw["w_p"], lw["w_g"], seq)
    q, kt = _conv(qk_pre, lw["conv_w"], lw["conv_b"], seq)
    gc, gr = _gate_prep(gcol, grow, lw["b_gates"])
    hf, hb = _mlstm(q, kt, v, gc, gr, seq)
    qa, ka, va = _mla_prep(cq, ckv, kra, krb, st["cos"], st["sin"], lw["qg"], lw["kvg"], lw["w_q"], lw["w_kv"])
    a_x, a_c = _attention(qa, ka, va, seq)
    x1, h2a, afft = _outproj(hf, hb, o, a_x, a_c, xs, modtab, lw["w_out"], lw["mn"], lw["ln1_g"], lw["ln1_b"],
                             lw["wr"], seq, with_ctx=not last)
    cap_x = EC_CAPACITY * seq // N_EXPERTS
    if last:
        return _moe_set(afft[:, :, :seq], h2a, x1, modtab, lw, cap_x, 0, False)
    cap_c = EC_CAPACITY * (s - seq) // N_EXPERTS
    x2_ctx = _moe_set(afft[:, :, seq:], h2a, x1, modtab, lw, cap_c, seq, True)
    return _moe_set(afft[:, :, :seq], h2a, x1, modtab, lw, cap_x, 0, False, tail=x2_ctx)


def kernel(x, c, ctx, c_ctx, w_mod, b_mod, w_in, b_gates, conv_w, conv_b, m_norm_w, q_norm_w, kv_norm_w, w_uq, w_ukv,
           w_out, ln1_g, ln1_b, w_router, w_gate, w_up, w_down, ln2_g, ln2_b):
    p = dict(x=x, c=c, ctx=ctx, c_ctx=c_ctx, w_mod=w_mod, b_mod=b_mod, w_in=w_in, b_gates=b_gates, conv_w=conv_w,
             conv_b=conv_b, m_norm_w=m_norm_w, q_norm_w=q_norm_w, kv_norm_w=kv_norm_w, w_uq=w_uq, w_ukv=w_ukv,
             w_out=w_out, ln1_g=ln1_g, ln1_b=ln1_b, w_router=w_router, w_gate=w_gate, w_up=w_up, w_down=w_down,
             ln2_g=ln2_g, ln2_b=ln2_b)
    st = _prepare(p)
    seq = x.shape[1]
    xs = jnp.concatenate([x, ctx], axis=1)
    nl = w_in.shape[0]
    for l in range(nl):
        xs = _layer(xs, st, l, seq, last=(l == nl - 1))
    return xs
```

```python
import functools
import math

import jax
import jax.numpy as jnp
from jax import lax
from jax.experimental import pallas as pl
from jax.experimental.pallas import tpu as pltpu

F32 = jnp.float32
BF16 = jnp.bfloat16
HIGHEST = lax.Precision.HIGHEST

D_MODEL = 1024
M_HEADS = 4
M_HEAD_DIM = 128
M_WIDTH = 512
M_CONV = 5
A_HEADS = 4
A_NOPE = 128
A_ROPE = 64
A_V = 128
A_WIDTH = 512
Q_LORA = 384
KV_LORA = 256
A_SCALE = (A_NOPE + A_ROPE) ** -0.5
Q_SCALE = A_SCALE * math.log2(math.e)
ROPE_BASE = 10000.0
GRID_W = 64
N_EXPERTS = 16
EXPERT_FF = 1024
EC_CAPACITY = 2
DEPTH = 2
ALPHA = (2 * DEPTH) ** 0.25
OFF_Q, OFF_K, OFF_V, OFF_O, OFF_G = 0, 512, 1024, 1536, 2048
OFF_CQ = OFF_G + 16
OFF_CKV = OFF_CQ + Q_LORA
OFF_KR = OFF_CKV + KV_LORA
IN_COLS = OFF_KR + A_ROPE

LANES = 128
SUBLANES = 8
VMEM_LIMIT = 56 * 1024 * 1024

P_QK = 0
P_V = 1024
P_O = 1536
P_CQ = 2048
P_CKV = 2432
P_KRA = 2688
P_KRB = 2816
P_G = 2944
P_COLS = 3072

ROW_TILE = 256
MCHUNK = 256


def _cparams(sem):
    return pltpu.CompilerParams(dimension_semantics=sem, vmem_limit_bytes=VMEM_LIMIT)


def _ln(x, eps):
    mu = jnp.mean(x, axis=-1, keepdims=True)
    xc = x - mu
    var = jnp.mean(xc * xc, axis=-1, keepdims=True)
    return xc * lax.rsqrt(var + eps)


def _silu(x):
    return x * (1.0 / (1.0 + jnp.exp(-x)))


def _mod_kernel(c_ref, w_ref, b_ref, o_ref):
    c = c_ref[...]
    a = _silu(c)
    o_ref[0] = jnp.dot(a, w_ref[0], precision=HIGHEST, preferred_element_type=F32) + b_ref[0]


def _modulation(cvec, w_mod, b_mod):
    nl, d, n6 = w_mod.shape
    r = cvec.shape[0]
    tn = 1536
    return pl.pallas_call(
        _mod_kernel,
        grid=(nl, n6 // tn),
        in_specs=[
            pl.BlockSpec((r, d), lambda l, j: (0, 0)),
            pl.BlockSpec((1, d, tn), lambda l, j: (l, 0, j)),
            pl.BlockSpec((1, 1, tn), lambda l, j: (l, 0, j)),
        ],
        out_specs=pl.BlockSpec((1, r, tn), lambda l, j: (l, 0, j)),
        out_shape=jax.ShapeDtypeStruct((nl, r, n6), F32),
        compiler_params=_cparams(("arbitrary", "arbitrary")),
        name="modulation",
    )(cvec, w_mod, b_mod.reshape(nl, 1, n6))


def _inproj_kernel(x_ref, mod_ref, w_ref, wg_ref, qk_ref, v_ref, o_ref, cq_ref, ckv_ref,
                   kra_ref, krb_ref, gcol_ref, grow_ref, *, n_xblk):
    i = pl.program_id(1)
    is_ctx = i >= n_xblk
    x = x_ref[0]
    shift = jnp.where(is_ctx, mod_ref[0, 1, 0:1, :], mod_ref[0, 0, 0:1, :])
    scale = jnp.where(is_ctx, mod_ref[0, 1, 1:2, :], mod_ref[0, 0, 1:2, :])
    h = (_ln(x, 1e-6) * (1.0 + scale) + shift).astype(BF16)
    p = jnp.dot(h, w_ref[...], preferred_element_type=F32)
    qk_ref[0] = p[:, P_QK:P_V]
    v_ref[0] = p[:, P_V:P_O].astype(BF16)
    o_ref[0] = p[:, P_O:P_CQ]
    cq_ref[0] = p[:, P_CQ:P_CKV]
    ckv_ref[0] = p[:, P_CKV:P_KRA]
    kra_ref[0] = p[:, P_KRA:P_KRB]
    krb_ref[0] = p[:, P_KRB:P_G]
    gcol_ref[0] = p[:, P_G:P_COLS]
    grow_ref[0] = lax.dot_general(wg_ref[...], h, (((1,), (1,)), ((), ())), preferred_element_type=F32)


def _inproj(xs, modtab, w_p, w_g, seq):
    b, s, d = xs.shape
    tm = ROW_TILE
    widths = [(1024, F32), (512, BF16), (512, F32), (Q_LORA, F32), (KV_LORA, F32), (128, F32), (128, F32),
              (128, F32)]
    out_shape = [jax.ShapeDtypeStruct((b, s, w), dt) for w, dt in widths]
    out_specs = [pl.BlockSpec((1, tm, w), lambda bi, i: (bi, i, 0)) for w, _ in widths]
    out_shape.append(jax.ShapeDtypeStruct((b, 16, s), F32))
    out_specs.append(pl.BlockSpec((1, 16, tm), lambda bi, i: (bi, 0, i)))
    return pl.pallas_call(
        functools.partial(_inproj_kernel, n_xblk=seq // tm),
        grid=(b, s // tm),
        in_specs=[
            pl.BlockSpec((1, tm, d), lambda bi, i: (bi, i, 0)),
            pl.BlockSpec((1, 2, 6, d), lambda bi, i: (bi, 0, 0, 0)),
            pl.BlockSpec((d, P_COLS), lambda bi, i: (0, 0)),
            pl.BlockSpec((16, d), lambda bi, i: (0, 0)),
        ],
        out_specs=out_specs,
        out_shape=out_shape,
        compiler_params=_cparams(("arbitrary", "arbitrary")),
        name="ln_inproj",
    )(xs, modtab, w_p, w_g)


def _conv_kernel(main_ref, prev_ref, next_ref, w_ref, b_ref, q_ref, kt_ref, ext_ref, *, tm, n_xblk, n_blk):
    i = pl.program_id(1)
    c = main_ref.shape[-1]
    first = jnp.logical_or(i == 0, i == n_xblk)
    last = jnp.logical_or(i == n_xblk - 1, i == n_blk - 1)
    zeros8 = jnp.zeros((SUBLANES, c), F32)
    ext_ref[0:SUBLANES, :] = jnp.where(first, zeros8, prev_ref[0])
    ext_ref[SUBLANES:SUBLANES + tm, :] = main_ref[0]
    ext_ref[SUBLANES + tm:, :] = jnp.where(last, zeros8, next_ref[0])
    acc = jnp.zeros((tm, c), F32) + b_ref[...]
    for j in range(M_CONV):
        off = SUBLANES - M_CONV // 2 + j
        acc = acc + ext_ref[off:off + tm, :] * w_ref[j:j + 1, :]
    y = _silu(acc)
    q_ref[0] = (y[:, :M_WIDTH] * (M_HEAD_DIM ** -0.5)).astype(BF16)
    kt_ref[0] = jnp.transpose(y[:, M_WIDTH:]).astype(BF16)


def _conv(qk_pre, conv_w, conv_b, seq):
    b, s, c = qk_pre.shape
    tm = ROW_TILE
    r8 = tm // SUBLANES
    n_blk = s // tm
    return pl.pallas_call(
        functools.partial(_conv_kernel, tm=tm, n_xblk=seq // tm, n_blk=n_blk),
        grid=(b, n_blk),
        in_specs=[
            pl.BlockSpec((1, tm, c), lambda bi, i: (bi, i, 0)),
            pl.BlockSpec((1, SUBLANES, c), lambda bi, i: (bi, jnp.maximum(i * r8 - 1, 0), 0)),
            pl.BlockSpec((1, SUBLANES, c), lambda bi, i: (bi, jnp.minimum((i + 1) * r8, s // SUBLANES - 1), 0)),
            pl.BlockSpec((SUBLANES, c), lambda bi, i: (0, 0)),
            pl.BlockSpec((1, c), lambda bi, i: (0, 0)),
        ],
        out_specs=[
            pl.BlockSpec((1, tm, M_WIDTH), lambda bi, i: (bi, i, 0)),
            pl.BlockSpec((1, M_WIDTH, tm), lambda bi, i: (bi, 0, i)),
        ],
        out_shape=[
            jax.ShapeDtypeStruct((b, s, M_WIDTH), BF16),
            jax.ShapeDtypeStruct((b, M_WIDTH, s), BF16),
        ],
        scratch_shapes=[pltpu.VMEM((tm + 2 * SUBLANES, c), F32)],
        compiler_params=_cparams(("arbitrary", "arbitrary")),
        name="conv_silu",
    )(qk_pre, qk_pre, qk_pre, conv_w, conv_b)


def _log_sigmoid(x):
    return jnp.minimum(x, 0.0) - jnp.log1p(jnp.exp(-jnp.abs(x)))


def _gate_kernel(gcol_ref, grow_ref, bcol_ref, brow_ref, ocol_ref, orow_ref, *, lc):
    nh = M_HEADS
    r = lax.broadcasted_iota(jnp.int32, (lc, lc), 0)
    c = lax.broadcasted_iota(jnp.int32, (lc, lc), 1)
    lower = (c <= r).astype(F32)
    upper = (c >= r).astype(F32)
    g = gcol_ref[0] + bcol_ref[...]
    lf = _log_sigmoid(g)
    lane = lax.broadcasted_iota(jnp.int32, g.shape, 1)
    lf_f = jnp.where(jnp.logical_and(lane >= nh, lane < 2 * nh), lf, 0.0)
    lf_b = jnp.where(jnp.logical_and(lane >= 3 * nh, lane < 4 * nh), lf, 0.0)
    cs_f = jnp.dot(lower, lf_f, precision=HIGHEST, preferred_element_type=F32)
    cs_b = jnp.dot(upper, lf_b, precision=HIGHEST, preferred_element_type=F32)
    ocol_ref[0] = cs_f + cs_b
    gr = grow_ref[0] + brow_ref[...]
    lfr = _log_sigmoid(gr)
    rs_f = jnp.dot(lfr, upper, precision=HIGHEST, preferred_element_type=F32)
    rs_b = jnp.dot(lfr, lower, precision=HIGHEST, preferred_element_type=F32)
    orow_ref[0, 0:nh, :] = gr[0:nh] - rs_f[nh:2 * nh]
    orow_ref[0, nh:2 * nh, :] = gr[2 * nh:3 * nh] - rs_b[3 * nh:4 * nh]
    orow_ref[0, 2 * nh:3 * nh, :] = rs_f[nh:2 * nh]
    orow_ref[0, 3 * nh:4 * nh, :] = rs_b[3 * nh:4 * nh]


def _gate_prep(gcol, grow, b_gates):
    b, s, _ = gcol.shape
    lc = MCHUNK
    bcol = jnp.zeros((1, LANES), F32).at[0, :16].set(b_gates)
    brow = jnp.broadcast_to(b_gates[:, None], (16, LANES)).astype(F32)
    return pl.pallas_call(
        functools.partial(_gate_kernel, lc=lc),
        grid=(b, s // lc),
        in_specs=[
            pl.BlockSpec((1, lc, LANES), lambda bi, i: (bi, i, 0)),
            pl.BlockSpec((1, 16, lc), lambda bi, i: (bi, 0, i)),
            pl.BlockSpec((1, LANES), lambda bi, i: (0, 0)),
            pl.BlockSpec((16, lc), lambda bi, i: (0, 0)),
        ],
        out_specs=[
            pl.BlockSpec((1, lc, LANES), lambda bi, i: (bi, i, 0)),
            pl.BlockSpec((1, 16, lc), lambda bi, i: (bi, 0, i)),
        ],
        out_shape=[
            jax.ShapeDtypeStruct((b, s, LANES), F32),
            jax.ShapeDtypeStruct((b, 16, s), F32),
        ],
        compiler_params=_cparams(("arbitrary", "arbitrary")),
        name="mlstm_gates",
    )(gcol, grow, bcol, jnp.broadcast_to(b_gates[:, None], (16, lc)).astype(F32))


def _mlstm_kernel(qf_ref, ktf_ref, vf_ref, gcf_ref, grf_ref, qb_ref, ktb_ref, vb_ref, gcb_ref, grb_ref,
                  hf_ref, hb_ref, c_ref, m_ref, *, lc):
    step = pl.program_id(1)
    nh, dh = M_HEADS, M_HEAD_DIM

    @pl.when(step == 0)
    def _():
        c_ref[...] = jnp.zeros(c_ref.shape, F32)
        m_ref[...] = jnp.zeros(m_ref.shape, F32)

    row_i = lax.broadcasted_iota(jnp.int32, (lc, lc), 0)
    col_j = lax.broadcasted_iota(jnp.int32, (lc, lc), 1)
    lane = lax.broadcasted_iota(jnp.int32, (lc, LANES), 1)
    ones_col = jnp.where(lane == 0, 1.0, 0.0).astype(BF16)
    dirs = ((qf_ref, ktf_ref, vf_ref, gcf_ref, grf_ref, hf_ref, col_j <= row_i),
            (qb_ref, ktb_ref, vb_ref, gcb_ref, grb_ref, hb_ref, col_j >= row_i))
    for d, (q_ref, kt_ref, v_ref, gc_ref, gr_ref, h_ref, mask) in enumerate(dirs):
        for hd in range(nh):
            sl = slice(hd * dh, (hd + 1) * dh)
            q = q_ref[0, :, sl]
            kt = kt_ref[0, sl, :]
            v = v_ref[0, :, sl]
            gi = (1 + 2 * d) * nh + hd
            b_col = gc_ref[0, :, gi:gi + 1]
            a_row = gr_ref[0, d * nh + hd:d * nh + hd + 1, :]
            b_row = gr_ref[0, (2 + d) * nh + hd:(2 + d) * nh + hd + 1, :]
            b_last = b_row[:, lc - 1:lc] if d == 0 else b_row[:, 0:1]
            sidx = d * nh + hd
            m_old = m_ref[sidx, :, 0:1]
            c_old = c_ref[sidx]
            d_log = jnp.where(mask, b_col + a_row, -jnp.inf)
            m_inter = b_col + m_old
            m_t = jnp.maximum(jnp.max(d_log, axis=1, keepdims=True), m_inter)
            s_mat = jnp.dot(q, kt, preferred_element_type=F32) * jnp.exp(d_log - m_t)
            w_inter = jnp.exp(m_inter - m_t)
            v_ext = jnp.concatenate([v, ones_col], axis=1)
            num_ext = (jnp.dot(s_mat.astype(BF16), v_ext, preferred_element_type=F32)
                       + w_inter * jnp.dot(q, c_old.astype(BF16), preferred_element_type=F32))
            den = num_ext[:, dh:dh + 1]
            h_ref[0, :, sl] = num_ext[:, :dh] / jnp.maximum(jnp.abs(den), jnp.exp(-m_t))
            g_row = b_last + a_row
            m_new = jnp.maximum(b_last + m_old, jnp.max(g_row, axis=1, keepdims=True))
            w_s = jnp.exp(g_row - m_new)
            decay = jnp.exp(b_last + m_old - m_new)
            kw = (kt.astype(F32) * w_s).astype(BF16)
            c_ref[sidx] = decay * c_old + jnp.dot(kw, v_ext, preferred_element_type=F32)
            m_ref[sidx] = jnp.broadcast_to(m_new, (1, LANES))


def _mlstm(q, kt, v, gcol, grow, seq):
    b, s, w = q.shape
    lc = MCHUNK
    nc, ncx = s // lc, seq // lc
    ncc = nc - ncx

    def fmap(c):
        return jnp.where(c < ncc, ncx + c, c - ncc)

    def bmap(c):
        return jnp.where(c < ncc, nc - 1 - c, ncx - 1 - (c - ncc))

    def specs(cmap):
        return [
            pl.BlockSpec((1, lc, w), lambda bi, c: (bi, cmap(c), 0)),
            pl.BlockSpec((1, w, lc), lambda bi, c: (bi, 0, cmap(c))),
            pl.BlockSpec((1, lc, w), lambda bi, c: (bi, cmap(c), 0)),
            pl.BlockSpec((1, lc, LANES), lambda bi, c: (bi, cmap(c), 0)),
            pl.BlockSpec((1, 16, lc), lambda bi, c: (bi, 0, cmap(c))),
        ]

    return pl.pallas_call(
        functools.partial(_mlstm_kernel, lc=lc),
        grid=(b, nc),
        in_specs=specs(fmap) + specs(bmap),
        out_specs=[
            pl.BlockSpec((1, lc, w), lambda bi, c: (bi, fmap(c), 0)),
            pl.BlockSpec((1, lc, w), lambda bi, c: (bi, bmap(c), 0)),
        ],
        out_shape=[jax.ShapeDtypeStruct((b, s, w), F32), jax.ShapeDtypeStruct((b, s, w), F32)],
        scratch_shapes=[
            pltpu.VMEM((2 * M_HEADS, M_HEAD_DIM, 2 * M_HEAD_DIM), F32),
            pltpu.VMEM((2 * M_HEADS, 1, LANES), F32),
        ],
        compiler_params=_cparams(("arbitrary", "arbitrary")),
        name="mlstm",
    )(q, kt, v, gcol, grow, q, kt, v, gcol, grow)


def _rms(x, g, eps=1e-6):
    return x * lax.rsqrt(jnp.mean(x * x, axis=-1, keepdims=True) + eps) * g


def _mla_prep_kernel(cq_ref, ckv_ref, kra_ref, krb_ref, cos_ref, sin_ref, qg_ref, kvg_ref, wq_ref, wkv_ref,
                     q_ref, k_ref, v_ref):
    nh, dn = A_HEADS, A_NOPE
    cos, sin = cos_ref[...], sin_ref[...]
    cqn = _rms(cq_ref[0], qg_ref[...]).astype(BF16)
    qa = jnp.dot(cqn, wq_ref[...], preferred_element_type=F32)
    ckvn = _rms(ckv_ref[0], kvg_ref[...]).astype(BF16)
    kva = jnp.dot(ckvn, wkv_ref[...], preferred_element_type=F32)
    kr = kra_ref[0] * cos + krb_ref[0] * sin
    lane = lax.broadcasted_iota(jnp.int32, kr.shape, 1)
    ones_col = jnp.where(lane == 0, 1.0, 0.0)
    for h in range(nh):
        qr = qa[:, (nh + h) * dn:(nh + h + 1) * dn] * cos + qa[:, (2 * nh + h) * dn:(2 * nh + h + 1) * dn] * sin
        q_ref[0, h] = (jnp.concatenate([qa[:, h * dn:(h + 1) * dn], qr], axis=1) * Q_SCALE).astype(BF16)
        k_ref[0, h] = jnp.concatenate([kva[:, 2 * h * dn:(2 * h + 1) * dn], kr], axis=1).astype(BF16)
        v_ref[0, h] = jnp.concatenate([kva[:, (2 * h + 1) * dn:(2 * h + 2) * dn], ones_col], axis=1).astype(BF16)


def _mla_prep(cq, ckv, kra, krb, cos_t, sin_t, qg, kvg, w_q, w_kv):
    b, s, _ = cq.shape
    tm = ROW_TILE
    nh = A_HEADS
    row = lambda w: pl.BlockSpec((1, tm, w), lambda bi, i: (bi, i, 0))
    const = lambda a: pl.BlockSpec(a.shape, lambda bi, i: (0, 0))
    head = lambda w: pl.BlockSpec((1, nh, tm, w), lambda bi, i: (bi, 0, i, 0))
    return pl.pallas_call(
        _mla_prep_kernel,
        grid=(b, s // tm),
        in_specs=[row(Q_LORA), row(KV_LORA), row(LANES), row(LANES),
                  pl.BlockSpec((tm, LANES), lambda bi, i: (i, 0)), pl.BlockSpec((tm, LANES), lambda bi, i: (i, 0)),
                  const(qg), const(kvg), const(w_q), const(w_kv)],
        out_specs=[head(2 * LANES), head(2 * LANES), head(2 * LANES)],
        out_shape=[jax.ShapeDtypeStruct((b, nh, s, 2 * LANES), BF16)] * 3,
        compiler_params=_cparams(("arbitrary", "arbitrary")),
        name="mla_prep",
    )(cq, ckv, kra, krb, cos_t, sin_t, qg, kvg, w_q, w_kv)


ATT_TQ = 1024
ATT_TK = 768


def _attn_kernel(q_ref, k_ref, v_ref, o_ref, *, tk, nk):
    q = q_ref[0, 0]
    tq = q.shape[0]
    m = jnp.full((tq, 1), -jnp.inf, F32)
    acc = jnp.zeros((tq, 2 * A_V), F32)
    for j in range(nk):
        k = k_ref[0, 0, j * tk:(j + 1) * tk, :]
        v = v_ref[0, 0, j * tk:(j + 1) * tk, :]
        s = lax.dot_general(q, k, (((1,), (1,)), ((), ())), preferred_element_type=F32)
        m_new = jnp.maximum(m, jnp.max(s, axis=1, keepdims=True))
        p = jnp.exp2(s - m_new)
        acc = jnp.exp2(m - m_new) * acc + jnp.dot(p.astype(BF16), v, preferred_element_type=F32)
        m = m_new
    o_ref[0] = (acc[:, :A_V] / acc[:, A_V:A_V + 1]).astype(BF16)


def _attention(q, k, v, seq):
    b, nh, s, dq = q.shape
    tq, tk = ATT_TQ, ATT_TK
    ctx_len = s - seq
    cblk = seq // ctx_len
    a_x = pl.pallas_call(
        functools.partial(_attn_kernel, tk=tk, nk=s // tk),
        grid=(b, nh, seq // tq),
        in_specs=[
            pl.BlockSpec((1, 1, tq, dq), lambda bi, h, i: (bi, h, i, 0)),
            pl.BlockSpec((1, 1, s, dq), lambda bi, h, i: (bi, h, 0, 0)),
            pl.BlockSpec((1, 1, s, dq), lambda bi, h, i: (bi, h, 0, 0)),
        ],
        out_specs=pl.BlockSpec((1, tq, A_V), lambda bi, h, i: (bi, i, h)),
        out_shape=jax.ShapeDtypeStruct((b, seq, nh * A_V), BF16),
        compiler_params=_cparams(("arbitrary", "arbitrary", "arbitrary")),
        name="mla_attention",
    )(q, k, v)
    ctx_spec = pl.BlockSpec((1, 1, ctx_len, dq), lambda bi, h: (bi, h, cblk, 0))
    a_c = pl.pallas_call(
        functools.partial(_attn_kernel, tk=ctx_len, nk=1),
        grid=(b, nh),
        in_specs=[ctx_spec, ctx_spec, ctx_spec],
        out_specs=pl.BlockSpec((1, ctx_len, A_V), lambda bi, h: (bi, 0, h)),
        out_shape=jax.ShapeDtypeStruct((b, ctx_len, nh * A_V), BF16),
        compiler_params=_cparams(("arbitrary", "arbitrary")),
        name="mla_attention_ctx",
    )(q, k, v)
    return a_x, a_c


def _outproj_kernel(hf_ref, hb_ref, o_ref, ax_ref, ac_ref, x_ref, mod_ref, wout_ref, mn_ref, g_ref, b_ref, wr_ref,
                    x1_ref, h2_ref, afft_ref, *, n_xblk):
    i = pl.program_id(1)
    is_ctx = i >= n_xblk
    mod = lambda k: jnp.where(is_ctx, mod_ref[0, 1, k:k + 1, :], mod_ref[0, 0, k:k + 1, :])
    hsum = hf_ref[0] + hb_ref[0]
    dh = M_HEAD_DIM
    hn = jnp.concatenate([_ln(hsum[:, h * dh:(h + 1) * dh], 1e-6) for h in range(M_HEADS)], axis=1)
    o = o_ref[0]
    m_out = hn * mn_ref[...] * (1.0 / (1.0 + jnp.exp(-o)))
    cat = jnp.concatenate([m_out.astype(BF16), jnp.where(is_ctx, ac_ref[0], ax_ref[0])], axis=1)
    y = jnp.dot(cat, wout_ref[...], preferred_element_type=F32)
    x1 = _ln(ALPHA * x_ref[0] + mod(2) * y, 1e-5) * g_ref[...] + b_ref[...]
    x1_ref[0] = x1
    h2 = _ln(x1, 1e-6) * (1.0 + mod(4)) + mod(3)
    logits = jnp.dot(h2, wr_ref[...], precision=HIGHEST, preferred_element_type=F32)
    lane = lax.broadcasted_iota(jnp.int32, logits.shape, 1)
    logits = jnp.where(lane < N_EXPERTS, logits, -jnp.inf)
    e = jnp.exp(logits - jnp.max(logits, axis=1, keepdims=True))
    aff = e / jnp.sum(e, axis=1, keepdims=True)
    h2_ref[0] = jnp.concatenate([h2, aff], axis=1)
    afft_ref[0] = jnp.transpose(aff)[:N_EXPERTS, :]


def _outproj(hf, hb, o, a_x, a_c, xs, modtab, w_out, mn, g, bb, wr, seq, with_ctx):
    b, s, d = xs.shape
    tm = ROW_TILE
    n_xblk = seq // tm
    if not with_ctx:
        s = seq
    row = lambda w: pl.BlockSpec((1, tm, w), lambda bi, i: (bi, i, 0))
    const = lambda a: pl.BlockSpec(a.shape, lambda bi, i: (0, 0))
    return pl.pallas_call(
        functools.partial(_outproj_kernel, n_xblk=n_xblk),
        grid=(b, s // tm),
        in_specs=[row(M_WIDTH), row(M_WIDTH), row(M_WIDTH),
                  pl.BlockSpec((1, tm, A_WIDTH), lambda bi, i: (bi, jnp.minimum(i, n_xblk - 1), 0)),
                  pl.BlockSpec((1, tm, A_WIDTH), lambda bi, i: (bi, jnp.maximum(i - n_xblk, 0), 0)),
                  row(d),
                  pl.BlockSpec((1, 2, 6, d), lambda bi, i: (bi, 0, 0, 0)),
                  const(w_out), const(mn), const(g), const(bb), const(wr)],
        out_specs=[row(d), row(d + LANES), pl.BlockSpec((1, N_EXPERTS, tm), lambda bi, i: (bi, 0, i))],
        out_shape=[jax.ShapeDtypeStruct((b, s, d), F32), jax.ShapeDtypeStruct((b, s, d + LANES), F32),
                   jax.ShapeDtypeStruct((b, N_EXPERTS, s), F32)],
        compiler_params=_cparams(("arbitrary", "arbitrary")),
        name="merge_outproj_norm_router",
    )(hf, hb, o, a_x, a_c, xs, modtab, w_out, mn, g, bb, wr)


def _seg(n):
    return -(-n // 1024) * 1024


def _route_kernel(aff_ref, idx_ref, sel_ref, cnt_ref, *, cap, nrow):
    ne = N_EXPERTS
    aff = aff_ref[0]
    bits = pltpu.bitcast(aff, jnp.int32)

    def count(mask):
        return jnp.sum(jnp.sum(mask.astype(F32), axis=1, keepdims=True), axis=2, keepdims=True)

    def bit_step(i, thr):
        cand = jnp.bitwise_or(thr, lax.shift_left(jnp.int32(1), 30 - i))
        return jnp.where(count(bits >= cand) >= cap, cand, thr)

    thr = lax.fori_loop(0, 31, bit_step, jnp.zeros((ne, 1, 1), jnp.int32))
    gt = bits > thr
    eq = bits == thr
    need = cap - count(gt)

    li = lax.broadcasted_iota(jnp.int32, (LANES, LANES), 0)
    lj = lax.broadcasted_iota(jnp.int32, (LANES, LANES), 1)
    tri = (li <= lj).astype(BF16)
    nr = ne * nrow
    shift = int(math.log2(nrow))
    bi = lax.broadcasted_iota(jnp.int32, (nr, nr), 0)
    bj = lax.broadcasted_iota(jnp.int32, (nr, nr), 1)
    same = lax.shift_right_logical(bi, shift) == lax.shift_right_logical(bj, shift)
    blk = jnp.logical_and(same, bj < bi).astype(BF16)

    def prefix(mask):
        m = mask.astype(BF16).reshape(nr, LANES)
        local = jnp.dot(m, tri, preferred_element_type=F32)
        tot = jnp.broadcast_to(local[:, LANES - 1:LANES], (nr, LANES)).astype(BF16)
        offs = jnp.dot(blk, tot, preferred_element_type=F32)
        return local, offs

    eq_local, eq_offs = prefix(eq)
    eq_f = eq.astype(F32)
    eq_rank = (eq_local + eq_offs).reshape(ne, nrow, LANES) - eq_f
    sel = jnp.logical_or(gt, jnp.logical_and(eq, eq_rank < need))
    local, offs = prefix(sel)
    cnt = (local + offs).reshape(ne, nrow, LANES)
    sel_ref[0] = sel.astype(BF16)
    cnt_ref[0] = cnt.astype(jnp.int32)

    rc = jnp.max(cnt, axis=2)
    rtot = jnp.sum(sel.astype(F32), axis=2)
    rc_excl = rc - rtot
    local3 = local.reshape(ne, nrow, LANES)
    p_r = lax.broadcasted_iota(jnp.int32, (cap, nrow), 0).astype(F32)
    r_r = lax.broadcasted_iota(jnp.int32, (cap, nrow), 1).astype(F32)
    p_l = lax.broadcasted_iota(jnp.int32, (cap, LANES), 0).astype(F32)
    lane = lax.broadcasted_iota(jnp.int32, (cap, LANES), 1)
    out = jnp.zeros((cap, LANES), jnp.int32)
    for e in range(ne):
        rowsel = jnp.sum((rc[e:e + 1, :] <= p_r).astype(F32), axis=1, keepdims=True)
        onehot = (r_r == rowsel)
        in_row = jnp.dot(onehot.astype(BF16), local3[e].astype(BF16), preferred_element_type=F32)
        before = jnp.sum(jnp.where(onehot, rc_excl[e:e + 1, :], 0.0), axis=1, keepdims=True)
        lane_of = jnp.sum((in_row + before <= p_l).astype(F32), axis=1, keepdims=True)
        tok = (rowsel * LANES + lane_of).astype(jnp.int32)
        out = jnp.where(lane == e, tok, out)
    idx_ref[0] = out


def _route(aff4, cap):
    b, ne, nrow, _ = aff4.shape
    return pl.pallas_call(
        functools.partial(_route_kernel, cap=cap, nrow=nrow),
        grid=(b,),
        in_specs=[pl.BlockSpec((1, ne, nrow, LANES), lambda bi: (bi, 0, 0, 0))],
        out_specs=[pl.BlockSpec((1, cap, LANES), lambda bi: (bi, 0, 0)),
                   pl.BlockSpec((1, ne, nrow, LANES), lambda bi: (bi, 0, 0, 0)),
                   pl.BlockSpec((1, ne, nrow, LANES), lambda bi: (bi, 0, 0, 0))],
        out_shape=[jax.ShapeDtypeStruct((b, cap, LANES), jnp.int32),
                   jax.ShapeDtypeStruct((b, ne, nrow, LANES), BF16),
                   jax.ShapeDtypeStruct((b, ne, nrow, LANES), jnp.int32)],
        compiler_params=_cparams(("arbitrary",)),
        name="ec_route",
    )(aff4)


def _expert_kernel(idx_hbm, h_hbm, wg_ref, wu_ref, wd_ref, y_ref, idx0, idx1, xin0, xin1, sem_i, sem_g, *,
                   cap, s_rows, row_off, nb):
    e = pl.program_id(0)
    b = pl.program_id(1)
    d = D_MODEL
    seg = idx0.shape[0]
    step = e * nb + b
    nsteps = N_EXPERTS * nb
    idx_bufs, x_bufs = (idx0, idx1), (xin0, xin1)

    def idx_copy(st, sl):
        eb = lax.rem(st, nb) * N_EXPERTS + st // nb
        return pltpu.make_async_copy(idx_hbm.at[pl.ds(pl.multiple_of(eb * seg, seg), seg)], idx_bufs[sl],
                                     sem_i.at[sl])

    def row_copy(st, sl, p):
        base = lax.rem(st, nb) * s_rows + row_off
        return pltpu.make_async_copy(h_hbm.at[pl.ds(base + idx_bufs[sl][p], 1)], x_bufs[sl].at[pl.ds(p, 1)],
                                     sem_g.at[sl])

    def rows_done(sl):
        return pltpu.make_async_copy(h_hbm.at[pl.ds(0, cap)], x_bufs[sl], sem_g.at[sl])

    @pl.when(step == 0)
    def _():
        idx_copy(0, 0).start()
        idx_copy(0, 0).wait()

        def body(p, c):
            row_copy(0, 0, p).start()
            return c

        lax.fori_loop(0, cap, body, 0)
        idx_copy(1, 1).start()

    def run(sl):
        nxt = lax.rem(step + 1, nsteps)
        rows_done(sl).wait()
        idx_copy(nxt, 1 - sl).wait()
        for p in range(cap):
            row_copy(nxt, 1 - sl, p).start(priority=p % 2)
        idx_copy(lax.rem(step + 2, nsteps), sl).start()
        x = x_bufs[sl][:, :d].astype(BF16)
        aff = x_bufs[sl][:, d:]
        lane = lax.broadcasted_iota(jnp.int32, aff.shape, 1)
        gate = jnp.sum(jnp.where(lane == e, aff, 0.0), axis=1, keepdims=True)
        hg = jnp.dot(x, wg_ref[0], preferred_element_type=F32)
        hu = jnp.dot(x, wu_ref[0], preferred_element_type=F32)
        act = (_silu(hg) * hu).astype(BF16)
        y_ref[0] = jnp.dot(act, wd_ref[0], preferred_element_type=F32) * gate

        @pl.when(step == nsteps - 1)
        def _():
            rows_done(1 - sl).wait()
            idx_copy(0, sl).wait()

    for sl in range(2):
        pl.when(lax.rem(step, 2) == sl)(functools.partial(run, sl))


def _experts(idx_tab, h2a, w_gate, w_up, w_down, cap, row_off):
    b, s, dw = h2a.shape
    ne, d, ff = w_gate.shape
    wspec = lambda r, c: pl.BlockSpec((1, r, c), lambda e, bi: (e, 0, 0))
    return pl.pallas_call(
        functools.partial(_expert_kernel, cap=cap, s_rows=s, row_off=row_off, nb=b),
        grid=(ne, b),
        in_specs=[pl.BlockSpec(memory_space=pl.ANY), pl.BlockSpec(memory_space=pl.ANY),
                  wspec(d, ff), wspec(d, ff), wspec(ff, d)],
        out_specs=pl.BlockSpec((1, cap, d), lambda e, bi: (bi, e, 0)),
        out_shape=jax.ShapeDtypeStruct((b, ne * cap, d), F32),
        scratch_shapes=[pltpu.SMEM((_seg(cap),), jnp.int32), pltpu.SMEM((_seg(cap),), jnp.int32),
                        pltpu.VMEM((cap, dw), F32), pltpu.VMEM((cap, dw), F32),
                        pltpu.SemaphoreType.DMA((2,)), pltpu.SemaphoreType.DMA((2,))],
        compiler_params=_cparams(("arbitrary", "arbitrary")),
        name="ec_experts",
    )(idx_tab, h2a.reshape(b * s, dw), w_gate, w_up, w_down)


COMBINE_TB = 256


def _combine_kernel(idx_hbm, bnd_hbm, y_hbm, sel_ref, x1_ref, mod_ref, g_ref, b_ref, *rest, cap, tb, is_ctx, nblk,
                    has_tail):
    if has_tail:
        tail_ref, o_ref, idx_smem, bnd_smem, buf_ref, sem_i, sem_g = rest
    else:
        o_ref, idx_smem, bnd_smem, buf_ref, sem_i, sem_g = rest
    b = pl.program_id(0)
    j = pl.program_id(1)
    ne = N_EXPERTS
    ybase = b * ne * cap

    def bounds(e, blk):
        return bnd_smem[e * (nblk + 1) + blk], bnd_smem[e * (nblk + 1) + blk + 1]

    def issue(blk, sl):
        for e in range(ne):
            lo, hi = bounds(e, blk)

            def body(p, c, e=e):
                tl = idx_smem[e * _seg(cap) + p] - blk * tb
                pltpu.make_async_copy(y_hbm.at[pl.ds(ybase + e * cap + p, 1)],
                                      buf_ref.at[sl, pl.ds(e * tb + tl, 1)], sem_g.at[sl]).start(priority=e % 2)
                return c

            lax.fori_loop(lo, hi, body, 0)

    def drain(blk, sl):
        n = bounds(0, blk)[1] - bounds(0, blk)[0]
        for e in range(1, ne):
            lo, hi = bounds(e, blk)
            n = n + hi - lo
        for bit in range((ne * tb).bit_length()):
            @pl.when(jnp.bitwise_and(lax.shift_right_logical(n, bit), 1) == 1)
            def _(bit=bit):
                rows = 1 << bit
                pltpu.make_async_copy(y_hbm.at[pl.ds(0, rows)], buf_ref.at[sl, pl.ds(0, rows)], sem_g.at[sl]).wait()

    @pl.when(jnp.logical_and(b == 0, j == 0))
    def _():
        buf_ref[...] = jnp.zeros(buf_ref.shape, F32)

    @pl.when(j == 0)
    def _():
        n_i, n_b = idx_smem.shape[0], bnd_smem.shape[0]
        c1 = pltpu.make_async_copy(idx_hbm.at[pl.ds(pl.multiple_of(b * n_i, n_i), n_i)], idx_smem, sem_i.at[0])
        c2 = pltpu.make_async_copy(bnd_hbm.at[pl.ds(pl.multiple_of(b * n_b, n_b), n_b)], bnd_smem, sem_i.at[1])
        c1.start()
        c2.start()
        c1.wait()
        c2.wait()
        issue(0, 0)

    @pl.when(j < nblk)
    def _():
        slot = lax.rem(j, 2)

        @pl.when(j + 1 < nblk)
        def _():
            issue(j + 1, 1 - slot)

        drain(j, slot)
        ri = lax.broadcasted_iota(jnp.int32, (tb, tb), 0)
        ci = lax.broadcasted_iota(jnp.int32, (tb, tb), 1)
        eye = (ri == ci).astype(BF16)
        mask_t = lax.dot_general(eye, sel_ref[0], (((1,), (1,)), ((), ())), preferred_element_type=F32)
        moe = jnp.zeros((tb, D_MODEL), F32)
        for e in range(ne):
            moe = moe + jnp.where(mask_t[:, e:e + 1] > 0.5, buf_ref[slot, e * tb:(e + 1) * tb, :], 0.0)
        k = 1 if is_ctx else 0
        o_ref[0] = _ln(ALPHA * x1_ref[0] + mod_ref[0, k, 5:6, :] * moe, 1e-5) * g_ref[...] + b_ref[...]

    if has_tail:
        @pl.when(j >= nblk)
        def _():
            o_ref[0] = tail_ref[0]


def _combine(idx_flat, bnd_flat, y_all, sel, x1, modtab, g, bb, cap, is_ctx, blk_off, tail=None):
    b, ne, n = sel.shape
    d = x1.shape[-1]
    tb = COMBINE_TB
    nblk = n // tb
    ntail = 0 if tail is None else tail.shape[1] // tb
    last = nblk - 1
    in_specs = [pl.BlockSpec(memory_space=pl.ANY), pl.BlockSpec(memory_space=pl.ANY), pl.BlockSpec(memory_space=pl.ANY),
                pl.BlockSpec((1, ne, tb), lambda bi, j: (bi, 0, jnp.minimum(j, last))),
                pl.BlockSpec((1, tb, d), lambda bi, j: (bi, jnp.minimum(j, last) + blk_off, 0)),
                pl.BlockSpec((1, 2, 6, d), lambda bi, j: (bi, 0, 0, 0)),
                pl.BlockSpec((1, d), lambda bi, j: (0, 0)), pl.BlockSpec((1, d), lambda bi, j: (0, 0))]
    args = [idx_flat, bnd_flat, y_all.reshape(b * ne * cap, d), sel, x1, modtab, g, bb]
    if tail is not None:
        in_specs.append(pl.BlockSpec((1, tb, d), lambda bi, j: (bi, jnp.maximum(j - nblk, 0), 0)))
        args.append(tail)
    return pl.pallas_call(
        functools.partial(_combine_kernel, cap=cap, tb=tb, is_ctx=is_ctx, nblk=nblk, has_tail=tail is not None),
        grid=(b, nblk + ntail),
        in_specs=in_specs,
        out_specs=pl.BlockSpec((1, tb, d), lambda bi, j: (bi, j, 0)),
        out_shape=jax.ShapeDtypeStruct((b, (nblk + ntail) * tb, d), F32),
        scratch_shapes=[pltpu.SMEM((ne * _seg(cap),), jnp.int32), pltpu.SMEM((_seg(ne * (nblk + 1)),), jnp.int32),
                        pltpu.VMEM((2, ne * tb, d), F32), pltpu.SemaphoreType.DMA((2,)),
                        pltpu.SemaphoreType.DMA((2,))],
        compiler_params=_cparams(("arbitrary", "arbitrary")),
        name="ec_combine_norm",
    )(*args)


def _moe_set(afft_set, h2a, x1, modtab, lw, cap, row_off, is_ctx, tail=None):
    b, ne, n = afft_set.shape
    n_pad = max(n, 16 * LANES)
    aff4 = jnp.pad(afft_set, ((0, 0), (0, 0), (0, n_pad - n)), constant_values=-1.0).reshape(b, ne, n_pad // LANES, LANES)
    idx, sel, cnt = _route(aff4, cap)
    idx_t = jnp.transpose(idx[:, :, :ne], (0, 2, 1))
    idx_flat = jnp.pad(idx_t, ((0, 0), (0, 0), (0, _seg(cap) - cap))).reshape(-1)
    nblk = n // COMBINE_TB
    rpb = COMBINE_TB // LANES
    ends = cnt[:, :, rpb - 1::rpb, LANES - 1][:, :, :nblk]
    bnd = jnp.concatenate([jnp.zeros((b, ne, 1), jnp.int32), ends], axis=2).reshape(b, -1)
    bnd_flat = jnp.pad(bnd, ((0, 0), (0, _seg(bnd.shape[1]) - bnd.shape[1]))).reshape(-1)
    y_all = _experts(idx_flat, h2a, lw["w_gate"], lw["w_up"], lw["w_down"], cap, row_off)
    sel_rows = sel.reshape(b, ne, n_pad)[:, :, :n]
    return _combine(idx_flat, bnd_flat, y_all, sel_rows, x1, modtab, lw["ln2_g"], lw["ln2_b"], cap, is_ctx,
                    row_off // COMBINE_TB, tail)


def _rope_tables(seq, ctx_len):
    rows = seq // GRID_W
    half = A_ROPE // 2
    inv = ROPE_BASE ** (-jnp.arange(0, half, 2, dtype=F32) / half)
    row = jnp.repeat(jnp.arange(rows, dtype=F32), GRID_W)
    col = jnp.tile(jnp.arange(GRID_W, dtype=F32), rows)
    ang = jnp.concatenate([row[:, None] * inv, col[:, None] * inv], axis=-1)
    cos, sin = jnp.cos(ang), jnp.sin(ang)
    pad = jnp.zeros((seq, LANES - A_ROPE), F32)
    cos_x = jnp.concatenate([cos, cos, pad], axis=1)
    sin_x = jnp.concatenate([sin, sin, pad], axis=1)
    cos_c = jnp.concatenate([jnp.ones((ctx_len, A_ROPE), F32), jnp.zeros((ctx_len, LANES - A_ROPE), F32)], axis=1)
    sin_c = jnp.zeros((ctx_len, LANES), F32)
    return jnp.concatenate([cos_x, cos_c], axis=0), jnp.concatenate([sin_x, sin_c], axis=0)


def _rot(w):
    half = A_ROPE // 2
    return jnp.concatenate([-w[..., half:], w[..., :half]], axis=-1)


def _prepare(p):
    c, c_ctx = p["c"], p["c_ctx"]
    b, d = c.shape
    nl = p["w_mod"].shape[0]
    rows = -(-(b + 1) // SUBLANES) * SUBLANES
    cvec = jnp.zeros((rows, d), F32).at[:b].set(c).at[b].set(c_ctx)
    mods = _modulation(cvec, p["w_mod"], p["b_mod"]).reshape(nl, rows, 6, d)
    mx = mods[:, :b]
    mc = jnp.broadcast_to(mods[:, b:b + 1], (nl, b, 6, d))
    modtab = jnp.stack([mx, mc], axis=2)
    layers = []
    for l in range(nl):
        w_in = p["w_in"][l]
        w_kr = w_in[:, OFF_KR:IN_COLS]
        z64 = jnp.zeros((d, LANES - A_ROPE), F32)
        w_p = jnp.concatenate([
            w_in[:, OFF_Q:OFF_G], w_in[:, OFF_CQ:OFF_KR], w_kr, z64, _rot(w_kr), z64,
            w_in[:, OFF_G:OFF_CQ], jnp.zeros((d, LANES - 16), F32)], axis=1).astype(BF16)
        w_uq = p["w_uq"][l].reshape(Q_LORA, A_HEADS, A_NOPE + A_ROPE)
        zq = jnp.zeros((Q_LORA, A_HEADS, LANES - A_ROPE), F32)
        w_q = jnp.concatenate([
            w_uq[:, :, :A_NOPE].reshape(Q_LORA, -1),
            jnp.concatenate([w_uq[:, :, A_NOPE:], zq], axis=2).reshape(Q_LORA, -1),
            jnp.concatenate([_rot(w_uq[:, :, A_NOPE:]), zq], axis=2).reshape(Q_LORA, -1)], axis=1).astype(BF16)
        w_router = p["w_router"][l]
        layers.append(dict(
            w_p=w_p,
            w_g=jnp.transpose(w_in[:, OFF_G:OFF_CQ]).astype(BF16),
            b_gates=p["b_gates"][l],
            conv_w=jnp.zeros((SUBLANES, 2 * M_WIDTH), F32).at[:M_CONV].set(p["conv_w"][l]),
            conv_b=p["conv_b"][l].reshape(1, -1),
            w_q=w_q,
            w_kv=p["w_ukv"][l].astype(BF16),
            qg=p["q_norm_w"][l].reshape(1, -1),
            kvg=p["kv_norm_w"][l].reshape(1, -1),
            w_out=p["w_out"][l].astype(BF16),
            mn=p["m_norm_w"][l].reshape(1, -1),
            ln1_g=p["ln1_g"][l].reshape(1, -1),
            ln1_b=p["ln1_b"][l].reshape(1, -1),
            wr=jnp.concatenate([w_router, jnp.zeros((d, LANES - N_EXPERTS), F32)], axis=1),
            w_gate=p["w_gate"][l].astype(BF16),
            w_up=p["w_up"][l].astype(BF16),
            w_down=p["w_down"][l].astype(BF16),
            ln2_g=p["ln2_g"][l].reshape(1, -1),
            ln2_b=p["ln2_b"][l].reshape(1, -1),
        ))
    cos_t, sin_t = _rope_tables(p["x"].shape[1], p["ctx"].shape[1])
    return dict(modtab=modtab, layers=layers, cos=cos_t, sin=sin_t)


def _layer(xs, st, l, seq, last):
    b, s, d = xs.shape
    lw, modtab = st["layers"][l], st["modtab"][l]
    qk_pre, v, o, cq, ckv, kra, krb, gcol, grow = _inproj(xs, modtab, lw["w_p"], lw["w_g"], seq)
    q, kt = _conv(qk_pre, lw["conv_w"], lw["conv_b"], seq)
    gc, gr = _gate_prep(gcol, grow, lw["b_gates"])
    hf, hb = _mlstm(q, kt, v, gc, gr, seq)
    qa, ka, va = _mla_prep(cq, ckv, kra, krb, st["cos"], st["sin"], lw["qg"], lw["kvg"], lw["w_q"], lw["w_kv"])
    a_x, a_c = _attention(qa, ka, va, seq)
    x1, h2a, afft = _outproj(hf, hb, o, a_x, a_c, xs, modtab, lw["w_out"], lw["mn"], lw["ln1_g"], lw["ln1_b"],
                             lw["wr"], seq, with_ctx=not last)
    cap_x = EC_CAPACITY * seq // N_EXPERTS
    if last:
        return _moe_set(afft[:, :, :seq], h2a, x1, modtab, lw, cap_x, 0, False)
    cap_c = EC_CAPACITY * (s - seq) // N_EXPERTS
    x2_ctx = _moe_set(afft[:, :, seq:], h2a, x1, modtab, lw, cap_c, seq, True)
    return _moe_set(afft[:, :, :seq], h2a, x1, modtab, lw, cap_x, 0, False, tail=x2_ctx)


def kernel(x, c, ctx, c_ctx, w_mod, b_mod, w_in, b_gates, conv_w, conv_b, m_norm_w, q_norm_w, kv_norm_w, w_uq, w_ukv,
           w_out, ln1_g, ln1_b, w_router, w_gate, w_up, w_down, ln2_g, ln2_b):
    p = dict(x=x, c=c, ctx=ctx, c_ctx=c_ctx, w_mod=w_mod, b_mod=b_mod, w_in=w_in, b_gates=b_gates, conv_w=conv_w,
             conv_b=conv_b, m_norm_w=m_norm_w, q_norm_w=q_norm_w, kv_norm_w=kv_norm_w, w_uq=w_uq, w_ukv=w_ukv,
             w_out=w_out, ln1_g=ln1_g, ln1_b=ln1_b, w_router=w_router, w_gate=w_gate, w_up=w_up, w_down=w_down,
             ln2_g=ln2_g, ln2_b=ln2_b)
    st = _prepare(p)
    seq = x.shape[1]
    xs = jnp.concatenate([x, ctx], axis=1)
    nl = w_in.shape[0]
    for l in range(nl):
        xs = _layer(xs, st, l, seq, last=(l == nl - 1))
    return xs
```
